```python
import math
import jax, jax.numpy as jnp
from jax import lax
import numpy as np

D_MODEL = 1024
BATCH = 1
SEQ = 16384
DEPTH = 4

N_MIXERS = 3
HEAD_DIM = 64
MIX_HEADS = 12
MIX_WIDTH = MIX_HEADS * HEAD_DIM
MEM_HEADS = 4
MEM_WIDTH = MEM_HEADS * HEAD_DIM
MEM_LEN = 256
D_FF = 2816
ROT_DIM = HEAD_DIM // 4
ROPE_THETA = 500000.0
Q_BLOCK = 128
CONV_WIDTH = 3
NSA_KV_HEADS = 4
NSA_GROUP = MIX_HEADS // NSA_KV_HEADS
NSA_KV_WIDTH = NSA_KV_HEADS * HEAD_DIM
CMP_LEN = 32
CMP_STRIDE = 16
SLC_LEN = 64
SLC_TOPN = 16
WINDOW = 512
EPS = 1e-6
N_LAYERS_FOX = len(range(0, DEPTH, N_MIXERS))
N_LAYERS_CONV = len(range(1, DEPTH, N_MIXERS))
N_LAYERS_NSA = len(range(2, DEPTH, N_MIXERS))
FOX_IN = 3 * MIX_WIDTH + MIX_HEADS + MEM_WIDTH
CONV_IN = 3 * MIX_WIDTH + MEM_WIDTH
NSA_IN = MIX_WIDTH + 6 * NSA_KV_WIDTH + 3 * MIX_HEADS + MEM_WIDTH

kernel_name = "hybrid_fox_shortconv_nsa_trunk"


def rmsnorm(x, g):
    xf = x.astype(jnp.float32)
    y = xf * lax.rsqrt(jnp.mean(xf * xf, axis=-1, keepdims=True) + EPS)
    return (y * g.astype(jnp.float32)).astype(x.dtype)


def swiglu(h, w_in, w_out):
    g, u = jnp.split(h @ w_in, 2, axis=-1)
    return (jax.nn.silu(g) * u) @ w_out


def split_cols(t, sizes):
    out, off = [], 0
    for s in sizes:
        out.append(t[..., off:off + s])
        off += s
    return out


def partial_rope(x, pos):
    half = ROT_DIM // 2
    inv = ROPE_THETA ** (-jnp.arange(half, dtype=jnp.float32) / half)
    ang = pos.astype(jnp.float32)[:, None] * inv[None, :]
    cos = jnp.cos(ang)[None, :, None, :].astype(x.dtype)
    sin = jnp.sin(ang)[None, :, None, :].astype(x.dtype)
    x1, x2 = x[..., :half], x[..., half:ROT_DIM]
    return jnp.concatenate([x1 * cos - x2 * sin, x2 * cos + x1 * sin, x[..., ROT_DIM:]], axis=-1)


def masked_softmax(s, mask):
    s = jnp.where(mask, s.astype(jnp.float32), -jnp.inf)
    m = jnp.max(s, axis=-1, keepdims=True)
    m = jnp.where(jnp.isfinite(m), m, 0.0)
    p = jnp.exp(s - m)
    return p / jnp.maximum(jnp.sum(p, axis=-1, keepdims=True), jnp.finfo(jnp.float32).tiny)


def fox_attention(q, k, v, f_logit, b_f):
    B, S, H, dh = q.shape
    nb = S // Q_BLOCK
    logf = jax.nn.log_sigmoid(f_logit.astype(jnp.float32) + b_f.astype(jnp.float32))
    F = jnp.cumsum(logf, axis=1).transpose(0, 2, 1)
    qh = q.transpose(0, 2, 1, 3) * (1.0 / math.sqrt(dh))
    kh = k.transpose(0, 2, 1, 3)
    vh = v.transpose(0, 2, 1, 3)
    q_blk = qh.reshape(B, H, nb, Q_BLOCK, dh).transpose(2, 0, 1, 3, 4)
    f_blk = F.reshape(B, H, nb, Q_BLOCK).transpose(2, 0, 1, 3)
    starts = jnp.arange(nb) * Q_BLOCK
    kpos = jnp.arange(S)

    def block(args):
        qb, fb, s0 = args
        tpos = s0 + jnp.arange(Q_BLOCK)
        s = jnp.einsum('bhtd,bhsd->bhts', qb, kh).astype(jnp.float32) + fb[..., None] - F[:, :, None, :]
        p = masked_softmax(s, kpos[None, :] <= tpos[:, None])
        return jnp.einsum('bhts,bhsd->bhtd', p.astype(vh.dtype), vh)

    o = lax.map(block, (q_blk, f_blk, starts))
    return o.transpose(1, 0, 3, 2, 4).reshape(B, S, H * dh)


def short_conv(b_gate, c_gate, v, w):
    u = c_gate * v
    y = lax.conv_general_dilated(u, w[:, None, :], window_strides=(1,), padding=[(CONV_WIDTH - 1, 0)],
                                 dimension_numbers=('NWC', 'WIO', 'NWC'), feature_group_count=u.shape[-1])
    return b_gate * y


def nsa_attention(q, kc, vc, ks, vs, kw, vw, gate_logit, cmp_pos, cmp_w1, cmp_w2, pos):
    B, S, H, dh = q.shape
    G = NSA_KV_HEADS
    dt = q.dtype
    nb = S // Q_BLOCK
    q = partial_rope(q, pos)
    kc = partial_rope(kc, pos)
    ks = partial_rope(ks, pos)
    kw = partial_rope(kw, pos)
    n_cmp = (S - CMP_LEN) // CMP_STRIDE + 1
    cmp_start = jnp.arange(n_cmp) * CMP_STRIDE
    widx = cmp_start[:, None] + jnp.arange(CMP_LEN)[None, :]

    def compress(t, pe, w1, w2):
        blk = t[:, widx] + pe[None, None, :, None, :]
        blk = blk.transpose(0, 1, 3, 2, 4).reshape(B, n_cmp, G, CMP_LEN * dh)
        return (jax.nn.silu(blk @ w1) @ w2).transpose(0, 2, 1, 3)

    k_cmp = compress(kc, cmp_pos[0], cmp_w1[0], cmp_w2[0])
    v_cmp = compress(vc, cmp_pos[1], cmp_w1[1], cmp_w2[1])
    cmp_end = cmp_start + CMP_LEN - 1
    n_slc = S // SLC_LEN
    n_top = min(SLC_TOPN, n_slc)
    slc_start = jnp.arange(n_slc) * SLC_LEN
    ks_blk = ks.transpose(0, 2, 1, 3).reshape(B, G, n_slc, SLC_LEN, dh)
    vs_blk = vs.transpose(0, 2, 1, 3).reshape(B, G, n_slc, SLC_LEN, dh)
    overlap = ((cmp_start[:, None] < slc_start[None, :] + SLC_LEN) &
               (cmp_start[:, None] + CMP_LEN > slc_start[None, :])).astype(jnp.float32)
    kw_pad = jnp.pad(kw.transpose(0, 2, 1, 3), ((0, 0), (0, 0), (WINDOW, 0), (0, 0)))
    vw_pad = jnp.pad(vw.transpose(0, 2, 1, 3), ((0, 0), (0, 0), (WINDOW, 0), (0, 0)))
    qg = (q * (1.0 / math.sqrt(dh))).reshape(B, S, G, NSA_GROUP, dh).transpose(0, 2, 3, 1, 4)
    q_blk = qg.reshape(B, G, NSA_GROUP, nb, Q_BLOCK, dh).transpose(3, 0, 1, 2, 4, 5)
    starts = jnp.arange(nb) * Q_BLOCK
    bi = jnp.arange(B)[:, None, None, None]
    gi = jnp.arange(G)[None, :, None, None]
    jblk = jnp.arange(n_slc)

    def block(args):
        qb, s0 = args
        tpos = s0 + jnp.arange(Q_BLOCK)
        sc = jnp.einsum('bgqtd,bgnd->bgqtn', qb, k_cmp)
        pc = masked_softmax(sc, cmp_end[None, :] <= tpos[:, None])
        oc = jnp.einsum('bgqtn,bgnd->bgqtd', pc.astype(dt), v_cmp)
        imp = jnp.einsum('bgqtn,nj->bgtj', pc, overlap)
        tblk = tpos // SLC_LEN
        valid = jblk[None, :] <= tblk[:, None]
        forced = (jblk[None, :] == 0) | (jblk[None, :] == tblk[:, None]) | (jblk[None, :] == tblk[:, None] - 1)
        score = jnp.where(valid, jnp.where(forced, jnp.inf, imp), -jnp.inf)
        top_val, top_idx = lax.top_k(score, n_top)
        kg = ks_blk[bi, gi, top_idx].reshape(B, G, Q_BLOCK, n_top * SLC_LEN, dh)
        vg = vs_blk[bi, gi, top_idx].reshape(B, G, Q_BLOCK, n_top * SLC_LEN, dh)
        kpos_s = (top_idx[..., None] * SLC_LEN + jnp.arange(SLC_LEN)).reshape(B, G, Q_BLOCK, n_top * SLC_LEN)
        ok = jnp.broadcast_to((top_val > -jnp.inf)[..., None], top_idx.shape + (SLC_LEN,)).reshape(kpos_s.shape)
        mask_s = (kpos_s <= tpos[None, None, :, None]) & ok
        ss = jnp.einsum('bgqtd,bgtnd->bgqtn', qb, kg)
        ps = masked_softmax(ss, mask_s[:, :, None])
        o_s = jnp.einsum('bgqtn,bgtnd->bgqtd', ps.astype(dt), vg)
        kwin = lax.dynamic_slice_in_dim(kw_pad, s0, WINDOW + Q_BLOCK, axis=2)
        vwin = lax.dynamic_slice_in_dim(vw_pad, s0, WINDOW + Q_BLOCK, axis=2)
        wpos = s0 - WINDOW + jnp.arange(WINDOW + Q_BLOCK)
        mask_w = (wpos[None, :] >= 0) & (wpos[None, :] <= tpos[:, None]) & (tpos[:, None] - wpos[None, :] < WINDOW)
        sw = jnp.einsum('bgqtd,bgnd->bgqtn', qb, kwin)
        pw = masked_softmax(sw, mask_w)
        o_w = jnp.einsum('bgqtn,bgnd->bgqtd', pw.astype(dt), vwin)
        return oc, o_s, o_w

    oc, o_s, o_w = lax.map(block, (q_blk, starts))

    def unblock(o):
        return o.transpose(1, 0, 4, 2, 3, 5).reshape(B, S, H, dh)

    g = jax.nn.sigmoid(gate_logit.astype(jnp.float32)).astype(dt)[..., None]
    o = g[:, :, 0] * unblock(oc) + g[:, :, 1] * unblock(o_s) + g[:, :, 2] * unblock(o_w)
    return o.reshape(B, S, H * dh)


def memory_cross(qx, mem_k, mem_v):
    B, S, _ = qx.shape
    q = qx.reshape(B, S, MEM_HEADS, HEAD_DIM)
    s = jnp.einsum('bshd,bmhd->bhsm', q, mem_k) * (1.0 / math.sqrt(HEAD_DIM))
    p = jax.nn.softmax(s.astype(jnp.float32), axis=-1).astype(mem_v.dtype)
    return jnp.einsum('bhsm,bmhd->bshd', p, mem_v).reshape(B, S, MEM_WIDTH)


def setup_inputs(seed: int = 0) -> dict:
    key = jax.random.key(seed)
    ks = jax.random.split(key, 24)

    def nrm(k, shape, scale):
        return jax.random.normal(k, shape, jnp.float32) * scale

    def gain(k, shape):
        return 1.0 + 0.02 * jax.random.normal(k, shape, jnp.float32)

    D = D_MODEL
    return {
        "x": nrm(ks[0], (BATCH, SEQ, D), 1.0),
        "mem": nrm(ks[1], (BATCH, MEM_LEN, D), 1.0),
        "ffn1_norm": gain(ks[2], (DEPTH, D)),
        "ffn1_w_in": nrm(ks[3], (DEPTH, D, 2 * D_FF), D ** -0.5),
        "ffn1_w_out": nrm(ks[4], (DEPTH, D_FF, D), D_FF ** -0.5),
        "mix_norm": gain(ks[5], (DEPTH, D)),
        "mix_w_out": nrm(ks[6], (DEPTH, MIX_WIDTH + MEM_WIDTH, D), (MIX_WIDTH + MEM_WIDTH) ** -0.5),
        "mem_norm": gain(ks[7], (D,)),
        "mem_w_kv": nrm(ks[8], (DEPTH, D, 2 * MEM_WIDTH), D ** -0.5),
        "fox_w_in": nrm(ks[9], (N_LAYERS_FOX, D, FOX_IN), D ** -0.5),
        "fox_b_f": 2.0 + 0.5 * jax.random.normal(ks[10], (N_LAYERS_FOX, MIX_HEADS), jnp.float32),
        "conv_w_in": nrm(ks[11], (N_LAYERS_CONV, D, CONV_IN), D ** -0.5),
        "conv_w": nrm(ks[12], (N_LAYERS_CONV, CONV_WIDTH, MIX_WIDTH), CONV_WIDTH ** -0.5),
        "nsa_w_in": nrm(ks[13], (N_LAYERS_NSA, D, NSA_IN), D ** -0.5),
        "nsa_cmp_pos": nrm(ks[14], (N_LAYERS_NSA, 2, CMP_LEN, HEAD_DIM), 0.1),
        "nsa_cmp_w1": nrm(ks[15], (N_LAYERS_NSA, 2, CMP_LEN * HEAD_DIM, HEAD_DIM), (CMP_LEN * HEAD_DIM) ** -0.5),
        "nsa_cmp_w2": nrm(ks[16], (N_LAYERS_NSA, 2, HEAD_DIM, HEAD_DIM), HEAD_DIM ** -0.5),
        "ffn2_norm": gain(ks[17], (DEPTH, D)),
        "ffn2_w_in": nrm(ks[18], (DEPTH, D, 2 * D_FF), D ** -0.5),
        "ffn2_w_out": nrm(ks[19], (DEPTH, D_FF, D), D_FF ** -0.5),
        "final_norm": gain(ks[20], (D,)),
    }


def reference(x, mem, ffn1_norm, ffn1_w_in, ffn1_w_out, mix_norm, mix_w_out, mem_norm, mem_w_kv,
              fox_w_in, fox_b_f, conv_w_in, conv_w, nsa_w_in, nsa_cmp_pos, nsa_cmp_w1, nsa_cmp_w2,
              ffn2_norm, ffn2_w_in, ffn2_w_out, final_norm):
    B, S, _ = x.shape
    M = mem.shape[1]
    pos = jnp.arange(S)
    mem_n = rmsnorm(mem, mem_norm)
    for i in range(DEPTH):
        x = x + 0.5 * swiglu(rmsnorm(x, ffn1_norm[i]), ffn1_w_in[i], ffn1_w_out[i])
        h = rmsnorm(x, mix_norm[i])
        kind, j = i % N_MIXERS, i // N_MIXERS
        if kind == 0:
            q, k, v, f, qx = split_cols(h @ fox_w_in[j], [MIX_WIDTH, MIX_WIDTH, MIX_WIDTH, MIX_HEADS, MEM_WIDTH])
            hd = (B, S, MIX_HEADS, HEAD_DIM)
            y_mix = fox_attention(q.reshape(hd), k.reshape(hd), v.reshape(hd), f, fox_b_f[j])
        elif kind == 1:
            bg, cg, v, qx = split_cols(h @ conv_w_in[j], [MIX_WIDTH, MIX_WIDTH, MIX_WIDTH, MEM_WIDTH])
            y_mix = short_conv(bg, cg, v, conv_w[j])
        else:
            q, kc, vc, ks_, vs_, kw, vw, gl, qx = split_cols(
                h @ nsa_w_in[j], [MIX_WIDTH] + [NSA_KV_WIDTH] * 6 + [3 * MIX_HEADS, MEM_WIDTH])
            kvd = (B, S, NSA_KV_HEADS, HEAD_DIM)
            y_mix = nsa_attention(q.reshape(B, S, MIX_HEADS, HEAD_DIM), kc.reshape(kvd), vc.reshape(kvd),
                                  ks_.reshape(kvd), vs_.reshape(kvd), kw.reshape(kvd), vw.reshape(kvd),
                                  gl.reshape(B, S, 3, MIX_HEADS), nsa_cmp_pos[j], nsa_cmp_w1[j],
                                  nsa_cmp_w2[j], pos)
        mem_kv = (mem_n @ mem_w_kv[i]).reshape(B, M, 2, MEM_HEADS, HEAD_DIM)
        y_mem = memory_cross(qx, mem_kv[:, :, 0], mem_kv[:, :, 1])
        x = x + jnp.concatenate([y_mix, y_mem], axis=-1) @ mix_w_out[i]
        x = x + 0.5 * swiglu(rmsnorm(x, ffn2_norm[i]), ffn2_w_in[i], ffn2_w_out[i])
    return rmsnorm(x, final_norm)
```

```python
import functools
import math

import jax
import jax.numpy as jnp
from jax import lax
from jax.experimental import pallas as pl
from jax.experimental.pallas import tpu as pltpu

F32 = jnp.float32
BF16 = jnp.bfloat16

HEAD_DIM = 64
MIX_HEADS = 12
MEM_HEADS = 4
N_MIXERS = 3
ROT_DIM = HEAD_DIM // 4
ROPE_THETA = 500000.0
CONV_WIDTH = 3
NSA_KV_HEADS = 4
NSA_GROUP = MIX_HEADS // NSA_KV_HEADS
CMP_LEN = 32
CMP_STRIDE = 16
SLC_LEN = 64
SLC_TOPN = 16
WINDOW = 512
EPS = 1e-6

LANE = 128
SUBLANE = 8
VMEM_LIMIT_BYTES = 56 * 2**20
NEG = -1e30
SCALE = 1.0 / math.sqrt(HEAD_DIM)


def _params(*sem):
    return pltpu.CompilerParams(dimension_semantics=sem, vmem_limit_bytes=VMEM_LIMIT_BYTES)


def _const_spec(shape):
    zeros = (0,) * len(shape)
    return pl.BlockSpec(shape, lambda *_: zeros, pipeline_mode=pl.Buffered(1))


def _rms_bf16(x, g):
    ms = jnp.mean(x * x, axis=-1, keepdims=True)
    return (x * lax.rsqrt(ms + EPS) * g).astype(BF16)


def _dot(a, b):
    return jnp.dot(a, b, preferred_element_type=F32)


def _dot_nt(a, b):
    return lax.dot_general(a, b, (((1,), (1,)), ((), ())), preferred_element_type=F32)


def _ffn_body(x_ref, g_ref, wg_ref, wu_ref, wo_ref, fg_ref, o_ref, hn_ref, acc_ref, *, final_norm):
    j = pl.program_id(1)

    @pl.when(j == 0)
    def _():
        hn_ref[...] = _rms_bf16(x_ref[...], g_ref[...])

    hn = hn_ref[...]
    gate = _dot(hn, wg_ref[...])
    up = _dot(hn, wu_ref[...])
    act = (gate * jax.nn.sigmoid(gate) * up).astype(BF16)
    part = _dot(act, wo_ref[...])

    @pl.when(j == 0)
    def _():
        acc_ref[...] = part

    @pl.when(j > 0)
    def _():
        acc_ref[...] += part

    @pl.when(j == pl.num_programs(1) - 1)
    def _():
        y = x_ref[...] + 0.5 * acc_ref[...]
        if final_norm:
            ms = jnp.mean(y * y, axis=-1, keepdims=True)
            y = y * lax.rsqrt(ms + EPS) * fg_ref[...]
        o_ref[...] = y


def _ffn(x, g, w_in, w_out, final_g=None, *, tm, fc):
    s, d = x.shape
    f = w_out.shape[0]
    nf = f // fc
    fg = jnp.ones((1, d), F32) if final_g is None else final_g.reshape(1, d)
    return pl.pallas_call(
        functools.partial(_ffn_body, final_norm=final_g is not None),
        grid=(s // tm, nf),
        in_specs=[
            pl.BlockSpec((tm, d), lambda i, j: (i, 0)),
            pl.BlockSpec((1, d), lambda i, j: (0, 0)),
            pl.BlockSpec((d, fc), lambda i, j: (0, j)),
            pl.BlockSpec((d, fc), lambda i, j: (0, nf + j)),
            pl.BlockSpec((fc, d), lambda i, j: (j, 0)),
            pl.BlockSpec((1, d), lambda i, j: (0, 0)),
        ],
        out_specs=pl.BlockSpec((tm, d), lambda i, j: (i, 0)),
        out_shape=jax.ShapeDtypeStruct((s, d), F32),
        scratch_shapes=[pltpu.VMEM((tm, d), BF16), pltpu.VMEM((tm, d), F32)],
        compiler_params=_params("parallel", "arbitrary"),
        name="ffn",
    )(x, g.reshape(1, d), w_in, w_in, w_out, fg)


def _proj_body(*refs, segs, has_rope, chunk):
    n_in = 6 if has_rope else 3
    x_ref, g_ref, w_ref = refs[:3]
    outs = refs[n_in:]
    hn = _rms_bf16(x_ref[...], g_ref[...])
    if has_rope:
        cos_t, sa_t, sb_t = (r[...] for r in refs[3:6])
    col = 0
    for o_ref, (width, rope) in zip(outs, segs):
        for c0 in range(0, width, chunk):
            cw = min(chunk, width - c0)
            y = _dot(hn, w_ref[:, col + c0:col + c0 + cw])
            if rope:
                rep = cw // LANE
                cos_c, sa_c, sb_c = (jnp.concatenate([t] * rep, axis=1) if rep > 1 else t
                                     for t in (cos_t, sa_t, sb_t))
                half = ROT_DIM // 2
                y = y * cos_c + pltpu.roll(y, half, axis=1) * sa_c + pltpu.roll(y, cw - half, axis=1) * sb_c
            o_ref[:, c0:c0 + cw] = y.astype(o_ref.dtype)
        col += width


def _proj(x, g, w, segs, rope_tables=None, *, tm, chunk=256):
    s, d = x.shape
    n = w.shape[1]
    assert n == sum(sg[0] for sg in segs)
    has_rope = rope_tables is not None
    in_specs = [
        pl.BlockSpec((tm, d), lambda i: (i, 0)),
        pl.BlockSpec((1, d), lambda i: (0, 0)),
        _const_spec((d, n)),
    ]
    args = [x, g.reshape(1, d), w]
    if has_rope:
        in_specs += [pl.BlockSpec((tm, LANE), lambda i: (i, 0))] * 3
        args += list(rope_tables)
    return pl.pallas_call(
        functools.partial(_proj_body, segs=tuple((sg[0], sg[2]) for sg in segs), has_rope=has_rope,
                          chunk=chunk),
        grid=(s // tm,),
        in_specs=in_specs,
        out_specs=[pl.BlockSpec((tm, sg[0]), lambda i: (i, 0)) for sg in segs],
        out_shape=[jax.ShapeDtypeStruct((s, sg[0]), sg[1]) for sg in segs],
        compiler_params=_params("parallel"),
        name="proj",
    )(*args)


def _fgate_body(f_ref, b_ref, o_ref, carry_ref):
    i = pl.program_id(0)

    @pl.when(i == 0)
    def _():
        carry_ref[...] = jnp.zeros_like(carry_ref)

    z = f_ref[...] + b_ref[...]
    x = jnp.minimum(z, 0.0) - jnp.log(1.0 + jnp.exp(-jnp.abs(z)))
    tm = x.shape[0]
    row = lax.broadcasted_iota(jnp.int32, x.shape, 0)
    sh = 1
    while sh < tm:
        x = x + jnp.where(row >= sh, pltpu.roll(x, sh, axis=0), 0.0)
        sh *= 2
    x = x + carry_ref[...]
    carry_ref[...] = x[tm - 1:tm, :]
    o_ref[0] = x.T[:2 * SUBLANE, :]


def _fgate(f, b, *, tk):
    s = f.shape[0]
    return pl.pallas_call(
        _fgate_body,
        grid=(s // tk,),
        in_specs=[pl.BlockSpec((tk, LANE), lambda i: (i, 0)), pl.BlockSpec((1, LANE), lambda i: (0, 0))],
        out_specs=pl.BlockSpec((1, 2 * SUBLANE, tk), lambda i: (i, 0, 0)),
        out_shape=jax.ShapeDtypeStruct((s // tk, 2 * SUBLANE, tk), F32),
        scratch_shapes=[pltpu.VMEM((1, LANE), F32)],
        compiler_params=_params("arbitrary"),
        name="fgate",
    )(f, b)


def _fox_body(q_ref, k_ref, v_ref, ft_ref, o_ref, m_ref, l_ref, acc_ref, *, tq):
    h = pl.program_id(0)
    qi = pl.program_id(1)
    q = q_ref[...] * SCALE
    f_ref_row = ft_ref[qi, pl.ds(h, 1), :]
    f0 = f_ref_row[:, 0:1]
    m_ref[...] = jnp.full_like(m_ref, NEG)
    l_ref[...] = jnp.zeros_like(l_ref)
    acc_ref[...] = jnp.zeros_like(acc_ref)

    def step(kb, causal):
        start = pl.multiple_of(kb * tq, tq)
        k = k_ref[pl.ds(start, tq), :]
        v = v_ref[pl.ds(start, tq), :]
        s = _dot_nt(q, k) + (f0 - ft_ref[kb, pl.ds(h, 1), :])
        if causal:
            r = lax.broadcasted_iota(jnp.int32, s.shape, 0)
            c = lax.broadcasted_iota(jnp.int32, s.shape, 1)
            s = jnp.where(c <= r, s, NEG)
        m_old = m_ref[...]
        m_new = jnp.maximum(m_old, jnp.max(s, axis=1, keepdims=True))
        alpha = jnp.exp(m_old - m_new)
        p = jnp.exp(s - m_new)
        l_ref[...] = alpha * l_ref[...] + jnp.sum(p, axis=1, keepdims=True)
        acc_ref[...] = alpha * acc_ref[...] + _dot(p.astype(BF16), v)
        m_ref[...] = m_new

    def loop_body(kb, carry):
        step(kb, False)
        return carry

    lax.fori_loop(0, qi, loop_body, 0)
    step(qi, True)
    o_ref[...] = (acc_ref[...] / l_ref[...]).astype(o_ref.dtype)


def _fox(q, k, v, ft, *, tq):
    s = q.shape[0]
    nh = q.shape[1] // LANE
    nq = s // tq
    return pl.pallas_call(
        functools.partial(_fox_body, tq=tq),
        grid=(nh, nq),
        in_specs=[
            pl.BlockSpec((tq, LANE), lambda h, i: (i, h)),
            pl.BlockSpec((s, LANE), lambda h, i: (0, h)),
            pl.BlockSpec((s, LANE), lambda h, i: (0, h)),
            _const_spec(ft.shape),
        ],
        out_specs=pl.BlockSpec((tq, LANE), lambda h, i: (i, h)),
        out_shape=jax.ShapeDtypeStruct((s, nh * LANE), BF16),
        scratch_shapes=[pltpu.VMEM((tq, 1), F32), pltpu.VMEM((tq, 1), F32), pltpu.VMEM((tq, LANE), F32)],
        compiler_params=_params("parallel", "arbitrary"),
        name="fox",
    )(q, k, v, ft)


def _conv_body(bg_ref, cg_ref, v_ref, cgh_ref, vh_ref, w_ref, o_ref):
    i = pl.program_id(0)
    u = cg_ref[...] * v_ref[...]
    halo = jnp.where(i > 0, cgh_ref[...] * vh_ref[...], 0.0)
    ext = jnp.concatenate([halo, u], axis=0)
    u1 = pltpu.roll(ext, 1, axis=0)[SUBLANE:, :]
    u2 = pltpu.roll(ext, 2, axis=0)[SUBLANE:, :]
    w = w_ref[...]
    y = w[0:1, :] * u2 + w[1:2, :] * u1 + w[2:3, :] * u
    o_ref[...] = (bg_ref[...] * y).astype(o_ref.dtype)


def _conv(bg, cg, v, w, *, tm):
    s, c = bg.shape
    w8 = jnp.zeros((SUBLANE, c), F32).at[:CONV_WIDTH].set(w)
    per = tm // SUBLANE
    main = pl.BlockSpec((tm, c), lambda i: (i, 0))
    halo = pl.BlockSpec((SUBLANE, c), lambda i: (jnp.maximum(i * per - 1, 0), 0))
    return pl.pallas_call(
        _conv_body,
        grid=(s // tm,),
        in_specs=[main, main, main, halo, halo, pl.BlockSpec((SUBLANE, c), lambda i: (0, 0))],
        out_specs=main,
        out_shape=jax.ShapeDtypeStruct((s, c), BF16),
        compiler_params=_params("parallel"),
        name="conv",
    )(bg, cg, v, cg, v, w8)


def _cmp_body(t_ref, w1_ref, pe_ref, w2_ref, on_ref, ot_ref):
    t = t_ref[0, 0]
    n16 = t.shape[0]
    half = w1_ref.shape[1] // 2
    a = _dot(t, w1_ref[0, :half, :])
    b = _dot(t, w1_ref[0, half:, :])
    pec = _dot(pe_ref[0], w1_ref[0])[0:1, :]
    pre = a + pltpu.roll(b, n16 - 1, axis=0) + pec
    hid = (pre * jax.nn.sigmoid(pre)).astype(BF16)
    out = _dot(hid, w2_ref[0])
    on_ref[0, 0] = out.astype(on_ref.dtype)
    ot_ref[0, 0] = out.T.astype(ot_ref.dtype)


def _cmp(t16, w1, pe8, w2d):
    _, g, n16, kk = t16.shape
    return pl.pallas_call(
        _cmp_body,
        grid=(2, g),
        in_specs=[
            pl.BlockSpec((1, 1, n16, kk), lambda a, b: (a, b, 0, 0)),
            pl.BlockSpec((1, 2 * kk, HEAD_DIM), lambda a, b: (a, 0, 0)),
            pl.BlockSpec((1, SUBLANE, 2 * kk), lambda a, b: (a, 0, 0)),
            pl.BlockSpec((1, HEAD_DIM, LANE), lambda a, b: (a, 0, 0)),
        ],
        out_specs=[
            pl.BlockSpec((1, 1, n16, LANE), lambda a, b: (a, b, 0, 0)),
            pl.BlockSpec((1, 1, LANE, n16), lambda a, b: (a, b, 0, 0)),
        ],
        out_shape=[
            jax.ShapeDtypeStruct((2, g, n16, LANE), BF16),
            jax.ShapeDtypeStruct((2, g, LANE, n16), BF16),
        ],
        compiler_params=_params("parallel", "parallel"),
        name="nsa_cmp",
    )(t16, w1, pe8, w2d)


def _stack_heads(q_blk):
    return jnp.concatenate([q_blk[:, j * LANE:(j + 1) * LANE] for j in range(NSA_GROUP)], axis=0) * SCALE


def _unstack_heads(o3):
    tq = o3.shape[0] // NSA_GROUP
    return jnp.concatenate([o3[j * tq:(j + 1) * tq, :] for j in range(NSA_GROUP)], axis=1)


def _cmpattn_body(q_ref, kc_ref, vct_ref, ov_ref, pm_ref, oc_ref, sel_ref, flag_ref, *, tq, n_top):
    qi = pl.program_id(1)
    t0 = qi * tq
    q3 = _stack_heads(q_ref[...])
    kc = kc_ref[0, 0]
    n16 = kc.shape[0]
    st = _dot_nt(kc, q3)
    n_idx = lax.broadcasted_iota(jnp.int32, st.shape, 0)
    lane = lax.broadcasted_iota(jnp.int32, st.shape, 1)
    tpos = t0 + (lane & (tq - 1))
    mask = n_idx * CMP_STRIDE + (CMP_LEN - 1) <= tpos
    st = jnp.where(mask, st, NEG)
    m = jnp.max(st, axis=0, keepdims=True)
    p = jnp.where(mask, jnp.exp(st - m), 0.0)
    l = jnp.sum(p, axis=0, keepdims=True)
    inv_l = 1.0 / jnp.maximum(l, jnp.finfo(F32).tiny)
    pn = p * inv_l
    oct_ = _dot(vct_ref[0, 0], p.astype(BF16)) * inv_l
    oc_ref[...] = _unstack_heads(oct_.T).astype(oc_ref.dtype)

    psum = pn[:, 0:tq]
    for j in range(1, NSA_GROUP):
        psum = psum + pn[:, j * tq:(j + 1) * tq]
    p_hi = psum.astype(BF16)
    p_lo = (psum - p_hi.astype(F32)).astype(BF16)
    ov = ov_ref[...]
    imp = _dot(ov, p_hi) + _dot(ov, p_lo)

    jf = lax.broadcasted_iota(jnp.int32, imp.shape, 0).astype(F32)
    tl = t0 + lax.broadcasted_iota(jnp.int32, imp.shape, 1)
    tblk = jnp.right_shift(tl, SLC_LEN.bit_length() - 1).astype(F32)
    valid = jf <= tblk
    forced = (jf == 0.0) | (jf == tblk) | (jf == tblk - 1.0)
    score = jnp.where(valid, jnp.where(forced, jnp.inf, imp), -jnp.inf)
    sel = jnp.zeros(imp.shape, F32)
    big = float(imp.shape[0])
    for _ in range(n_top):
        mx = jnp.max(score, axis=0, keepdims=True)
        idx = jnp.min(jnp.where(score == mx, jf, big), axis=0, keepdims=True)
        hit = jf == idx
        sel = jnp.where(hit & (mx > -jnp.inf), 1.0, sel)
        score = jnp.where(hit, -jnp.inf, score)
    sel_ref[0, 0] = sel
    cnt = _dot_nt(jnp.ones((SUBLANE, tq), F32), sel)
    flag_ref[0, 0] = (_dot(cnt, pm_ref[...]) > 0.0).astype(jnp.int32)


def _cmpattn(q, kcmp, vcmpt, ov, pm, *, tq, n_top):
    s = q.shape[0]
    g, n16 = kcmp.shape[1], kcmp.shape[2]
    n_slc = ov.shape[0]
    nq = s // tq
    gw = NSA_GROUP * LANE
    return pl.pallas_call(
        functools.partial(_cmpattn_body, tq=tq, n_top=n_top),
        grid=(g, nq),
        in_specs=[
            pl.BlockSpec((tq, gw), lambda a, i: (i, a)),
            pl.BlockSpec((1, 1, n16, LANE), lambda a, i: (0, a, 0, 0)),
            pl.BlockSpec((1, 1, LANE, n16), lambda a, i: (1, a, 0, 0)),
            _const_spec(ov.shape),
            _const_spec(pm.shape),
        ],
        out_specs=[
            pl.BlockSpec((tq, gw), lambda a, i: (i, a)),
            pl.BlockSpec((1, 1, n_slc, tq), lambda a, i: (a, i, 0, 0)),
            pl.BlockSpec((1, 1, SUBLANE, LANE), lambda a, i: (a, i, 0, 0)),
        ],
        out_shape=[
            jax.ShapeDtypeStruct((s, g * gw), F32),
            jax.ShapeDtypeStruct((g, nq, n_slc, tq), F32),
            jax.ShapeDtypeStruct((g, nq, SUBLANE, LANE), jnp.int32),
        ],
        compiler_params=_params("parallel", "arbitrary"),
        name="nsa_cmpattn",
    )(q, kcmp, vcmpt, ov, pm)


def _selattn_body(flag_ref, q_ref, k_ref, vt_ref, sel_ref, o_ref, m_ref, l_ref, acc_ref, *, tq):
    qi = pl.program_id(1)
    q3 = _stack_heads(q_ref[...])
    m_ref[...] = jnp.full_like(m_ref, NEG)
    l_ref[...] = jnp.zeros_like(l_ref)
    acc_ref[...] = jnp.zeros_like(acc_ref)
    half = LANE // 2

    def chunk(c, causal):
        start = pl.multiple_of(c * LANE, LANE)
        k = k_ref[pl.ds(start, LANE), :]
        st = _dot_nt(k, q3)
        rows = []
        for b in range(2):
            srow = sel_ref[0, 0, pl.ds(2 * c + b, 1), :]
            brow = (jnp.concatenate([srow] * NSA_GROUP, axis=1) - 1.0) * (-NEG)
            rows.append(jnp.broadcast_to(brow, (half, brow.shape[1])))
        st = st + jnp.concatenate(rows, axis=0)
        if causal:
            kpos = lax.broadcasted_iota(jnp.int32, st.shape, 0)
            tpos = lax.broadcasted_iota(jnp.int32, st.shape, 1) & (tq - 1)
            st = jnp.where(kpos <= tpos, st, NEG)
        m_old = m_ref[...]
        m_new = jnp.maximum(m_old, jnp.max(st, axis=0, keepdims=True))
        alpha = jnp.exp(m_old - m_new)
        p = jnp.exp(st - m_new)
        l_ref[...] = alpha * l_ref[...] + jnp.sum(p, axis=0, keepdims=True)
        acc_ref[...] = alpha * acc_ref[...] + _dot(vt_ref[0, c], p.astype(BF16))
        m_ref[...] = m_new

    def loop_body(c, carry):
        @pl.when(flag_ref[0, 0, 0, c] > 0)
        def _():
            chunk(c, False)
        return carry

    lax.fori_loop(0, qi, loop_body, 0)
    chunk(qi, True)
    o_ref[...] = _unstack_heads((acc_ref[...] / l_ref[...]).T).astype(o_ref.dtype)


def _selattn(flags, q, ks, vst, sel, *, tq):
    s = q.shape[0]
    g, nq, n_slc, _ = sel.shape
    nchunk = vst.shape[1]
    gw = NSA_GROUP * LANE
    return pl.pallas_call(
        functools.partial(_selattn_body, tq=tq),
        grid=(g, nq),
        in_specs=[
            pl.BlockSpec((1, 1, SUBLANE, LANE), lambda a, i: (a, i, 0, 0), memory_space=pltpu.SMEM),
            pl.BlockSpec((tq, gw), lambda a, i: (i, a)),
            pl.BlockSpec((s, LANE), lambda a, i: (0, a)),
            pl.BlockSpec((1, nchunk, LANE, LANE), lambda a, i: (a, 0, 0, 0)),
            pl.BlockSpec((1, 1, n_slc, tq), lambda a, i: (a, i, 0, 0)),
        ],
        out_specs=pl.BlockSpec((tq, gw), lambda a, i: (i, a)),
        out_shape=jax.ShapeDtypeStruct((s, g * gw), F32),
        scratch_shapes=[pltpu.VMEM((1, gw), F32), pltpu.VMEM((1, gw), F32), pltpu.VMEM((LANE, gw), F32)],
        compiler_params=_params("parallel", "arbitrary"),
        name="nsa_selattn",
    )(flags, q, ks, vst, sel)


def _winattn_body(*refs, tq, nwb):
    q_ref = refs[0]
    k_refs = refs[1:2 + nwb]
    v_refs = refs[2 + nwb:3 + 2 * nwb]
    o_ref = refs[3 + 2 * nwb]
    qi = pl.program_id(1)
    q3 = _stack_heads(q_ref[...])
    k = jnp.concatenate([r[...] for r in k_refs], axis=0)
    v = jnp.concatenate([r[...] for r in v_refs], axis=0)
    s = _dot_nt(q3, k)
    tpos = qi * tq + (lax.broadcasted_iota(jnp.int32, s.shape, 0) & (tq - 1))
    kpos = (qi - nwb) * tq + lax.broadcasted_iota(jnp.int32, s.shape, 1)
    mask = (kpos >= 0) & (kpos <= tpos) & (tpos - kpos < WINDOW)
    s = jnp.where(mask, s, NEG)
    m = jnp.max(s, axis=1, keepdims=True)
    p = jnp.where(mask, jnp.exp(s - m), 0.0)
    l = jnp.sum(p, axis=1, keepdims=True)
    o3 = _dot(p.astype(BF16), v) / l
    o_ref[...] = _unstack_heads(o3).astype(o_ref.dtype)


def _winattn(q, kw, vw, *, tq):
    s = q.shape[0]
    g = kw.shape[1] // LANE
    nwb = WINDOW // tq
    gw = NSA_GROUP * LANE

    def kv_spec(back):
        return pl.BlockSpec((tq, LANE), lambda a, i: (jnp.maximum(i - back, 0), a))

    kv_specs = [kv_spec(nwb - b) for b in range(nwb + 1)]
    return pl.pallas_call(
        functools.partial(_winattn_body, tq=tq, nwb=nwb),
        grid=(g, s // tq),
        in_specs=[pl.BlockSpec((tq, gw), lambda a, i: (i, a))] + kv_specs + kv_specs,
        out_specs=pl.BlockSpec((tq, gw), lambda a, i: (i, a)),
        out_shape=jax.ShapeDtypeStruct((s, g * gw), F32),
        compiler_params=_params("parallel", "parallel"),
        name="nsa_winattn",
    )(q, *([kw] * (nwb + 1)), *([vw] * (nwb + 1)))


def _outproj_body(*refs, mode):
    if mode == "nsa":
        x_ref, oc_ref, os_ref, ow_ref, gl_ref, qx_ref, mk_ref, mv_ref, wmix_ref, wmem_ref, o_ref = refs
        gate = jax.nn.sigmoid(gl_ref[...])
        parts = []
        for hd in range(MIX_HEADS):
            sl = slice(hd * LANE, (hd + 1) * LANE)
            acc = None
            for b, br in enumerate((oc_ref, os_ref, ow_ref)):
                c = b * MIX_HEADS + hd
                term = gate[:, c:c + 1] * br[:, sl]
                acc = term if acc is None else acc + term
            parts.append(acc.astype(BF16))
        ymix = jnp.concatenate(parts, axis=1)
    else:
        x_ref, ymix_ref, qx_ref, mk_ref, mv_ref, wmix_ref, wmem_ref, o_ref = refs
        ymix = ymix_ref[...]
    y = _dot(ymix, wmix_ref[...])
    for hd in range(MEM_HEADS):
        sl = slice(hd * LANE, (hd + 1) * LANE)
        s = _dot_nt(qx_ref[:, sl], mk_ref[:, sl]) * SCALE
        m = jnp.max(s, axis=1, keepdims=True)
        p = jnp.exp(s - m)
        p = p / jnp.sum(p, axis=1, keepdims=True)
        ymem = _dot(p.astype(BF16), mv_ref[:, sl]).astype(BF16)
        y = y + _dot(ymem, wmem_ref[sl, :])
    o_ref[...] = x_ref[...] + y


def _outproj(x, mix_args, qx, mk, mv, wmix, wmem, *, mode, tm):
    s, d = x.shape
    row = lambda w: pl.BlockSpec((tm, w), lambda i: (i, 0))
    mix_specs = [row(a.shape[1]) for a in mix_args]
    return pl.pallas_call(
        functools.partial(_outproj_body, mode=mode),
        grid=(s // tm,),
        in_specs=[row(d)] + mix_specs + [row(qx.shape[1]), _const_spec(mk.shape), _const_spec(mv.shape),
                                         _const_spec(wmix.shape), _const_spec(wmem.shape)],
        out_specs=row(d),
        out_shape=jax.ShapeDtypeStruct((s, d), F32),
        compiler_params=_params("parallel"),
        name="outproj",
    )(x, *mix_args, qx, mk, mv, wmix, wmem)


def _pad_heads_cols(w, n_heads):
    d = w.shape[0]
    w = w.reshape(d, n_heads, HEAD_DIM)
    return jnp.pad(w, ((0, 0), (0, 0), (0, LANE - HEAD_DIM))).reshape(d, n_heads * LANE)


def _pad_heads_rows(w, n_heads):
    d = w.shape[1]
    w = w.reshape(n_heads, HEAD_DIM, d)
    return jnp.pad(w, ((0, 0), (0, LANE - HEAD_DIM), (0, 0))).reshape(n_heads * LANE, d)


def _pad_cols(w, width):
    return jnp.pad(w, ((0, 0), (0, width - w.shape[1])))


def _rope_tables(s):
    half = ROT_DIM // 2
    inv = ROPE_THETA ** (-jnp.arange(half, dtype=F32) / half)
    ang = jnp.arange(s, dtype=F32)[:, None] * inv[None, :]
    cos, sin = jnp.cos(ang), jnp.sin(ang)
    z = lambda n: jnp.zeros((s, n), F32)
    cos_t = jnp.concatenate([cos, cos, jnp.ones((s, LANE - ROT_DIM), F32)], axis=1)
    sa_t = jnp.concatenate([z(half), sin, z(LANE - ROT_DIM)], axis=1)
    sb_t = jnp.concatenate([-sin, z(LANE - half)], axis=1)
    return cos_t, sa_t, sb_t


def _pick(n, candidates):
    for c in candidates:
        if n % c == 0:
            return c
    raise ValueError(f"no tile size for {n}")


def kernel(x, mem, ffn1_norm, ffn1_w_in, ffn1_w_out, mix_norm, mix_w_out, mem_norm, mem_w_kv, fox_w_in,
           fox_b_f, conv_w_in, conv_w, nsa_w_in, nsa_cmp_pos, nsa_cmp_w1, nsa_cmp_w2, ffn2_norm, ffn2_w_in,
           ffn2_w_out, final_norm):
    b, s, d = x.shape
    assert b == 1
    depth = ffn1_norm.shape[0]
    mix_w = MIX_HEADS * HEAD_DIM
    mem_w = MEM_HEADS * HEAD_DIM
    kv_w = NSA_KV_HEADS * HEAD_DIM
    d_ff = ffn1_w_out.shape[1]
    tm_ffn = _pick(s, (1024, 512, 256, 128))
    fc = _pick(d_ff, (256, 128))
    tm = _pick(s, (512, 256, 128))
    tq_fox = _pick(s, (512, 256, 128))
    tq_nsa = 128
    tq_win = _pick(s, (256, 128))
    m_len = mem.shape[1]

    xs = x[0]
    mem_s = mem[0]
    rope_t = None
    for i in range(depth):
        xs = _ffn(xs, ffn1_norm[i], ffn1_w_in[i].astype(BF16), ffn1_w_out[i].astype(BF16), tm=tm_ffn, fc=fc)

        wkv = mem_w_kv[i]
        wkv_p = jnp.concatenate([_pad_heads_cols(wkv[:, :mem_w], MEM_HEADS),
                                 _pad_heads_cols(wkv[:, mem_w:], MEM_HEADS)], axis=1).astype(BF16)
        mk, mv = _proj(mem_s, mem_norm, wkv_p, [(MEM_HEADS * LANE, BF16, False)] * 2, tm=m_len)

        w_out = mix_w_out[i]
        wmem = _pad_heads_rows(w_out[mix_w:], MEM_HEADS).astype(BF16)
        kind, j = i % N_MIXERS, i // N_MIXERS
        if kind == 0:
            w = fox_w_in[j]
            wq, wk, wv = (w[:, a * mix_w:(a + 1) * mix_w] for a in range(3))
            wf = w[:, 3 * mix_w:3 * mix_w + MIX_HEADS]
            wqx = w[:, 3 * mix_w + MIX_HEADS:]
            wp = jnp.concatenate([_pad_heads_cols(wq, MIX_HEADS), _pad_heads_cols(wk, MIX_HEADS),
                                  _pad_heads_cols(wv, MIX_HEADS), _pad_heads_cols(wqx, MEM_HEADS),
                                  _pad_cols(wf, LANE)], axis=1).astype(BF16)
            hw = MIX_HEADS * LANE
            q, k, v, qx, f = _proj(xs, mix_norm[i], wp,
                                   [(hw, BF16, False)] * 3 + [(MEM_HEADS * LANE, BF16, False), (LANE, F32, False)],
                                   tm=tm)
            ft = _fgate(f, _pad_cols(fox_b_f[j].reshape(1, MIX_HEADS), LANE), tk=tq_fox)
            ymix = _fox(q, k, v, ft, tq=tq_fox)
            wmix = _pad_heads_rows(w_out[:mix_w], MIX_HEADS).astype(BF16)
            xs = _outproj(xs, [ymix], qx, mk, mv, wmix, wmem, mode="plain", tm=tm)
        elif kind == 1:
            w = conv_w_in[j]
            wp = jnp.concatenate([w[:, :3 * mix_w], _pad_heads_cols(w[:, 3 * mix_w:], MEM_HEADS)],
                                 axis=1).astype(BF16)
            bg, cg, v, qx = _proj(xs, mix_norm[i], wp,
                                  [(mix_w, F32, False)] * 3 + [(MEM_HEADS * LANE, BF16, False)], tm=tm)
            ymix = _conv(bg, cg, v, conv_w[j], tm=tm)
            xs = _outproj(xs, [ymix], qx, mk, mv, w_out[:mix_w].astype(BF16), wmem, mode="plain", tm=tm)
        else:
            w = nsa_w_in[j]
            off = mix_w
            wq = w[:, :mix_w]
            kvs = []
            for _ in range(6):
                kvs.append(_pad_heads_cols(w[:, off:off + kv_w], NSA_KV_HEADS))
                off += kv_w
            wgl = w[:, off:off + 3 * MIX_HEADS]
            wqx = w[:, off + 3 * MIX_HEADS:]
            wp = jnp.concatenate([_pad_heads_cols(wq, MIX_HEADS)] + kvs +
                                 [_pad_heads_cols(wqx, MEM_HEADS), _pad_cols(wgl, LANE)], axis=1).astype(BF16)
            if rope_t is None:
                rope_t = _rope_tables(s)
            gwid = NSA_KV_HEADS * LANE
            segs = [(MIX_HEADS * LANE, BF16, True)]
            segs += [(gwid, BF16, a % 2 == 0) for a in range(6)]
            segs += [(MEM_HEADS * LANE, BF16, False), (LANE, F32, False)]
            q, kc, vc, ks_, vs_, kw, vw, qx, gl = _proj(xs, mix_norm[i], wp, segs, rope_t, tm=tm)

            n16 = s // CMP_STRIDE
            sub = CMP_STRIDE

            def to_t16(t):
                t = t.reshape(n16, sub, NSA_KV_HEADS, LANE)[..., :HEAD_DIM]
                return t.transpose(2, 0, 1, 3).reshape(NSA_KV_HEADS, n16, sub * HEAD_DIM)

            t16 = jnp.stack([to_t16(kc), to_t16(vc)])
            w1 = nsa_cmp_w1[j].astype(BF16)
            pe8 = jnp.zeros((2, SUBLANE, CMP_LEN * HEAD_DIM), F32).at[:, 0].set(
                nsa_cmp_pos[j].reshape(2, CMP_LEN * HEAD_DIM)).astype(BF16)
            w2d = jnp.pad(nsa_cmp_w2[j], ((0, 0), (0, 0), (0, LANE - HEAD_DIM))).astype(BF16)
            cmp_n, cmp_t = _cmp(t16, w1, pe8, w2d)

            n_slc = s // SLC_LEN
            n_top = min(SLC_TOPN, n_slc)
            jj = jnp.arange(n_slc)[:, None] * SLC_LEN
            nn = jnp.arange(n16)[None, :] * CMP_STRIDE
            ov = ((nn < jj + SLC_LEN) & (nn + CMP_LEN > jj) & (nn + CMP_LEN <= s)).astype(BF16)
            n_chunk = s // LANE
            blocks_per_chunk = LANE // SLC_LEN
            pm = (jnp.arange(n_slc)[:, None] // blocks_per_chunk == jnp.arange(LANE)[None, :]).astype(F32)
            assert n_chunk <= LANE
            oc, sel, flags = _cmpattn(q, cmp_n, cmp_t, ov, pm, tq=tq_nsa, n_top=n_top)

            vst = vs_.reshape(n_chunk, LANE, NSA_KV_HEADS, LANE).transpose(2, 0, 3, 1)
            osel = _selattn(flags, q, ks_, vst, sel, tq=tq_nsa)
            ow = _winattn(q, kw, vw, tq=tq_win)
            wmix = _pad_heads_rows(w_out[:mix_w], MIX_HEADS).astype(BF16)
            xs = _outproj(xs, [oc, osel, ow, gl], qx, mk, mv, wmix, wmem, mode="nsa", tm=tm)

        last = i == depth - 1
        xs = _ffn(xs, ffn2_norm[i], ffn2_w_in[i].astype(BF16), ffn2_w_out[i].astype(BF16),
                  final_norm if last else None, tm=tm_ffn, fc=fc)
    return xs[None]
```

```python
import functools
import math

import jax
import jax.numpy as jnp
from jax import lax
from jax.experimental import pallas as pl
from jax.experimental.pallas import tpu as pltpu

F32 = jnp.float32
BF16 = jnp.bfloat16

HEAD_DIM = 64
MIX_HEADS = 12
MEM_HEADS = 4
N_MIXERS = 3
ROT_DIM = HEAD_DIM // 4
ROPE_THETA = 500000.0
CONV_WIDTH = 3
NSA_KV_HEADS = 4
NSA_GROUP = MIX_HEADS // NSA_KV_HEADS
CMP_LEN = 32
CMP_STRIDE = 16
SLC_LEN = 64
SLC_TOPN = 16
WINDOW = 512
EPS = 1e-6

LANE = 128
SUBLANE = 8
VMEM_LIMIT_BYTES = 56 * 2**20
NEG = -1e30
SCALE = 1.0 / math.sqrt(HEAD_DIM)


def _params(*sem):
    return pltpu.CompilerParams(dimension_semantics=sem, vmem_limit_bytes=VMEM_LIMIT_BYTES)


def _const_spec(shape):
    zeros = (0,) * len(shape)
    return pl.BlockSpec(shape, lambda *_: zeros, pipeline_mode=pl.Buffered(1))


def _rms_bf16(x, g):
    ms = jnp.mean(x * x, axis=-1, keepdims=True)
    return (x * lax.rsqrt(ms + EPS) * g).astype(BF16)


def _dot(a, b):
    return jnp.dot(a, b, preferred_element_type=F32)


def _dot_nt(a, b):
    return lax.dot_general(a, b, (((1,), (1,)), ((), ())), preferred_element_type=F32)


def _ffn_body(x_ref, g_ref, wg_ref, wu_ref, wo_ref, fg_ref, o_ref, hn_ref, acc_ref, *, final_norm):
    j = pl.program_id(1)

    @pl.when(j == 0)
    def _():
        hn_ref[...] = _rms_bf16(x_ref[...], g_ref[...])

    hn = hn_ref[...]
    gate = _dot(hn, wg_ref[...])
    up = _dot(hn, wu_ref[...])
    act = (gate * jax.nn.sigmoid(gate) * up).astype(BF16)
    part = _dot(act, wo_ref[...])

    @pl.when(j == 0)
    def _():
        acc_ref[...] = part

    @pl.when(j > 0)
    def _():
        acc_ref[...] += part

    @pl.when(j == pl.num_programs(1) - 1)
    def _():
        y = x_ref[...] + 0.5 * acc_ref[...]
        if final_norm:
            ms = jnp.mean(y * y, axis=-1, keepdims=True)
            y = y * lax.rsqrt(ms + EPS) * fg_ref[...]
        o_ref[...] = y


def _ffn(x, g, w_in, w_out, final_g=None, *, tm, fc):
    s, d = x.shape
    f = w_out.shape[0]
    nf = f // fc
    fg = jnp.ones((1, d), F32) if final_g is None else final_g.reshape(1, d)
    return pl.pallas_call(
        functools.partial(_ffn_body, final_norm=final_g is not None),
        grid=(s // tm, nf),
        in_specs=[
            pl.BlockSpec((tm, d), lambda i, j: (i, 0)),
            pl.BlockSpec((1, d), lambda i, j: (0, 0)),
            pl.BlockSpec((d, fc), lambda i, j: (0, j)),
            pl.BlockSpec((d, fc), lambda i, j: (0, nf + j)),
            pl.BlockSpec((fc, d), lambda i, j: (j, 0)),
            pl.BlockSpec((1, d), lambda i, j: (0, 0)),
        ],
        out_specs=pl.BlockSpec((tm, d), lambda i, j: (i, 0)),
        out_shape=jax.ShapeDtypeStruct((s, d), F32),
        scratch_shapes=[pltpu.VMEM((tm, d), BF16), pltpu.VMEM((tm, d), F32)],
        compiler_params=_params("parallel", "arbitrary"),
        name="ffn",
    )(x, g.reshape(1, d), w_in, w_in, w_out, fg)


def _proj_body(*refs, segs, has_rope, chunk):
    n_in = 6 if has_rope else 3
    x_ref, g_ref, w_ref = refs[:3]
    outs = refs[n_in:]
    hn = _rms_bf16(x_ref[...], g_ref[...])
    if has_rope:
        cos_t, sa_t, sb_t = (r[...] for r in refs[3:6])
    col = 0
    for o_ref, (width, rope) in zip(outs, segs):
        for c0 in range(0, width, chunk):
            cw = min(chunk, width - c0)
            y = _dot(hn, w_ref[:, col + c0:col + c0 + cw])
            if rope:
                rep = cw // LANE
                cos_c, sa_c, sb_c = (jnp.concatenate([t] * rep, axis=1) if rep > 1 else t
                                     for t in (cos_t, sa_t, sb_t))
                half = ROT_DIM // 2
                y = y * cos_c + pltpu.roll(y, half, axis=1) * sa_c + pltpu.roll(y, cw - half, axis=1) * sb_c
            o_ref[:, c0:c0 + cw] = y.astype(o_ref.dtype)
        col += width


def _proj(x, g, w, segs, rope_tables=None, *, tm, chunk=256):
    s, d = x.shape
    n = w.shape[1]
    assert n == sum(sg[0] for sg in segs)
    has_rope = rope_tables is not None
    in_specs = [
        pl.BlockSpec((tm, d), lambda i: (i, 0)),
        pl.BlockSpec((1, d), lambda i: (0, 0)),
        _const_spec((d, n)),
    ]
    args = [x, g.reshape(1, d), w]
    if has_rope:
        in_specs += [pl.BlockSpec((tm, LANE), lambda i: (i, 0))] * 3
        args += list(rope_tables)
    return pl.pallas_call(
        functools.partial(_proj_body, segs=tuple((sg[0], sg[2]) for sg in segs), has_rope=has_rope,
                          chunk=chunk),
        grid=(s // tm,),
        in_specs=in_specs,
        out_specs=[pl.BlockSpec((tm, sg[0]), lambda i: (i, 0)) for sg in segs],
        out_shape=[jax.ShapeDtypeStruct((s, sg[0]), sg[1]) for sg in segs],
        compiler_params=_params("parallel"),
        name="proj",
    )(*args)


def _fgate_body(f_ref, b_ref, o_ref, carry_ref):
    i = pl.program_id(0)

    @pl.when(i == 0)
    def _():
        carry_ref[...] = jnp.zeros_like(carry_ref)

    z = f_ref[...] + b_ref[...]
    x = jnp.minimum(z, 0.0) - jnp.log(1.0 + jnp.exp(-jnp.abs(z)))
    tm = x.shape[0]
    row = lax.broadcasted_iota(jnp.int32, x.shape, 0)
    sh = 1
    while sh < tm:
        x = x + jnp.where(row >= sh, pltpu.roll(x, sh, axis=0), 0.0)
        sh *= 2
    x = x + carry_ref[...]
    carry_ref[...] = x[tm - 1:tm, :]
    o_ref[0] = x.T[:2 * SUBLANE, :]


def _fgate(f, b, *, tk):
    s = f.shape[0]
    return pl.pallas_call(
        _fgate_body,
        grid=(s // tk,),
        in_specs=[pl.BlockSpec((tk, LANE), lambda i: (i, 0)), pl.BlockSpec((1, LANE), lambda i: (0, 0))],
        out_specs=pl.BlockSpec((1, 2 * SUBLANE, tk), lambda i: (i, 0, 0)),
        out_shape=jax.ShapeDtypeStruct((s // tk, 2 * SUBLANE, tk), F32),
        scratch_shapes=[pltpu.VMEM((1, LANE), F32)],
        compiler_params=_params("arbitrary"),
        name="fgate",
    )(f, b)


UNDERFLOW = 110.0


def _fox_body(q_ref, k_ref, v_ref, ft_ref, fend_ref, o_ref, m_ref, l_ref, acc_ref, kmax2_ref, *, tq):
    h = pl.program_id(0)
    qi = pl.program_id(1)
    q = q_ref[...] * SCALE
    f_ref_row = ft_ref[qi, pl.ds(h, 1), :]
    f0 = f_ref_row[:, 0:1]
    m_ref[...] = jnp.full_like(m_ref, NEG)
    l_ref[...] = jnp.zeros_like(l_ref)
    acc_ref[...] = jnp.zeros_like(acc_ref)

    @pl.when(qi == 0)
    def _():
        def kmax_body(c, mx):
            kf = k_ref[pl.ds(pl.multiple_of(c * tq, tq), tq), :].astype(F32)
            return jnp.maximum(mx, jnp.sum(kf * kf, axis=1, keepdims=True))

        mx = lax.fori_loop(0, k_ref.shape[0] // tq, kmax_body, jnp.zeros((tq, 1), F32))
        kmax2_ref[...] = jnp.max(mx, axis=0, keepdims=True)

    qf = q.astype(F32)
    qmax2 = jnp.max(jnp.sum(qf * qf, axis=1, keepdims=True), axis=0, keepdims=True)
    bound = f0 + (UNDERFLOW + 2.0 * jnp.sqrt(qmax2 * kmax2_ref[...]))
    fend = fend_ref[pl.ds(h, 1), :]
    blk = lax.broadcasted_iota(jnp.int32, fend.shape, 1)
    kb_lo = jnp.sum(jnp.where((fend > bound) & (blk < qi), 1, 0))

    def step(kb, causal):
        start = pl.multiple_of(kb * tq, tq)
        k = k_ref[pl.ds(start, tq), :]
        v = v_ref[pl.ds(start, tq), :]
        s = _dot_nt(q, k) + (f0 - ft_ref[kb, pl.ds(h, 1), :])
        if causal:
            r = lax.broadcasted_iota(jnp.int32, s.shape, 0)
            c = lax.broadcasted_iota(jnp.int32, s.shape, 1)
            s = jnp.where(c <= r, s, NEG)
        m_old = m_ref[...]
        m_new = jnp.maximum(m_old, jnp.max(s, axis=1, keepdims=True))
        alpha = jnp.exp(m_old - m_new)
        p = jnp.exp(s - m_new)
        l_ref[...] = alpha * l_ref[...] + jnp.sum(p, axis=1, keepdims=True)
        acc_ref[...] = alpha * acc_ref[...] + _dot(p.astype(BF16), v)
        m_ref[...] = m_new

    def loop_body(kb, carry):
        step(kb, False)
        return carry

    lax.fori_loop(kb_lo, qi, loop_body, 0)
    step(qi, True)
    o_ref[...] = (acc_ref[...] / l_ref[...]).astype(o_ref.dtype)


def _fox(q, k, v, ft, *, tq):
    s = q.shape[0]
    nh = q.shape[1] // LANE
    nq = s // tq
    assert nq <= LANE
    fend = jnp.pad(ft[:, :, tq - 1].T, ((0, 0), (0, LANE - nq)))
    return pl.pallas_call(
        functools.partial(_fox_body, tq=tq),
        grid=(nh, nq),
        in_specs=[
            pl.BlockSpec((tq, LANE), lambda h, i: (i, h)),
            pl.BlockSpec((s, LANE), lambda h, i: (0, h)),
            pl.BlockSpec((s, LANE), lambda h, i: (0, h)),
            _const_spec(ft.shape),
            _const_spec(fend.shape),
        ],
        out_specs=pl.BlockSpec((tq, LANE), lambda h, i: (i, h)),
        out_shape=jax.ShapeDtypeStruct((s, nh * LANE), BF16),
        scratch_shapes=[pltpu.VMEM((tq, 1), F32), pltpu.VMEM((tq, 1), F32), pltpu.VMEM((tq, LANE), F32),
                        pltpu.VMEM((1, 1), F32)],
        compiler_params=_params("arbitrary", "arbitrary"),
        name="fox",
    )(q, k, v, ft, fend)


def _conv_body(bg_ref, cg_ref, v_ref, cgh_ref, vh_ref, w_ref, o_ref):
    i = pl.program_id(0)
    u = cg_ref[...] * v_ref[...]
    halo = jnp.where(i > 0, cgh_ref[...] * vh_ref[...], 0.0)
    ext = jnp.concatenate([halo, u], axis=0)
    u1 = pltpu.roll(ext, 1, axis=0)[SUBLANE:, :]
    u2 = pltpu.roll(ext, 2, axis=0)[SUBLANE:, :]
    w = w_ref[...]
    y = w[0:1, :] * u2 + w[1:2, :] * u1 + w[2:3, :] * u
    o_ref[...] = (bg_ref[...] * y).astype(o_ref.dtype)


def _conv(bg, cg, v, w, *, tm):
    s, c = bg.shape
    w8 = jnp.zeros((SUBLANE, c), F32).at[:CONV_WIDTH].set(w)
    per = tm // SUBLANE
    main = pl.BlockSpec((tm, c), lambda i: (i, 0))
    halo = pl.BlockSpec((SUBLANE, c), lambda i: (jnp.maximum(i * per - 1, 0), 0))
    return pl.pallas_call(
        _conv_body,
        grid=(s // tm,),
        in_specs=[main, main, main, halo, halo, pl.BlockSpec((SUBLANE, c), lambda i: (0, 0))],
        out_specs=main,
        out_shape=jax.ShapeDtypeStruct((s, c), BF16),
        compiler_params=_params("parallel"),
        name="conv",
    )(bg, cg, v, cg, v, w8)


def _cmp_body(t_ref, w1_ref, pe_ref, w2_ref, on_ref, ot_ref):
    t = t_ref[0, 0]
    n16 = t.shape[0]
    half = w1_ref.shape[1] // 2
    a = _dot(t, w1_ref[0, :half, :])
    b = _dot(t, w1_ref[0, half:, :])
    pec = _dot(pe_ref[0], w1_ref[0])[0:1, :]
    pre = a + pltpu.roll(b, n16 - 1, axis=0) + pec
    hid = (pre * jax.nn.sigmoid(pre)).astype(BF16)
    out = _dot(hid, w2_ref[0])
    on_ref[0, 0] = out.astype(on_ref.dtype)
    ot_ref[0, 0] = out.T.astype(ot_ref.dtype)


def _cmp(t16, w1, pe8, w2d):
    _, g, n16, kk = t16.shape
    return pl.pallas_call(
        _cmp_body,
        grid=(2, g),
        in_specs=[
            pl.BlockSpec((1, 1, n16, kk), lambda a, b: (a, b, 0, 0)),
            pl.BlockSpec((1, 2 * kk, HEAD_DIM), lambda a, b: (a, 0, 0)),
            pl.BlockSpec((1, SUBLANE, 2 * kk), lambda a, b: (a, 0, 0)),
            pl.BlockSpec((1, HEAD_DIM, LANE), lambda a, b: (a, 0, 0)),
        ],
        out_specs=[
            pl.BlockSpec((1, 1, n16, LANE), lambda a, b: (a, b, 0, 0)),
            pl.BlockSpec((1, 1, LANE, n16), lambda a, b: (a, b, 0, 0)),
        ],
        out_shape=[
            jax.ShapeDtypeStruct((2, g, n16, LANE), BF16),
            jax.ShapeDtypeStruct((2, g, LANE, n16), BF16),
        ],
        compiler_params=_params("parallel", "parallel"),
        name="nsa_cmp",
    )(t16, w1, pe8, w2d)


def _stack_heads(q_blk):
    return jnp.concatenate([q_blk[:, j * LANE:(j + 1) * LANE] for j in range(NSA_GROUP)], axis=0) * SCALE


def _unstack_heads(o3):
    tq = o3.shape[0] // NSA_GROUP
    return jnp.concatenate([o3[j * tq:(j + 1) * tq, :] for j in range(NSA_GROUP)], axis=1)


def _cmpattn_body(q_ref, kc_ref, vct_ref, ov_ref, pm_ref, oc_ref, sel_ref, flag_ref, *, tq, n_top):
    qi = pl.program_id(1)
    t0 = qi * tq
    q3 = _stack_heads(q_ref[...])
    kc = kc_ref[0, 0]
    n16 = kc.shape[0]
    st = _dot_nt(kc, q3)
    n_idx = lax.broadcasted_iota(jnp.int32, st.shape, 0)
    lane = lax.broadcasted_iota(jnp.int32, st.shape, 1)
    tpos = t0 + (lane & (tq - 1))
    mask = n_idx * CMP_STRIDE + (CMP_LEN - 1) <= tpos
    st = jnp.where(mask, st, NEG)
    m = jnp.max(st, axis=0, keepdims=True)
    p = jnp.where(mask, jnp.exp(st - m), 0.0)
    l = jnp.sum(p, axis=0, keepdims=True)
    inv_l = 1.0 / jnp.maximum(l, jnp.finfo(F32).tiny)
    pn = p * inv_l
    oct_ = _dot(vct_ref[0, 0], p.astype(BF16)) * inv_l
    oc_ref[...] = _unstack_heads(oct_.T).astype(oc_ref.dtype)

    psum = pn[:, 0:tq]
    for j in range(1, NSA_GROUP):
        psum = psum + pn[:, j * tq:(j + 1) * tq]
    p_hi = psum.astype(BF16)
    p_lo = (psum - p_hi.astype(F32)).astype(BF16)
    ov = ov_ref[...]
    imp = _dot(ov, p_hi) + _dot(ov, p_lo)

    jf = lax.broadcasted_iota(jnp.int32, imp.shape, 0).astype(F32)
    tl = t0 + lax.broadcasted_iota(jnp.int32, imp.shape, 1)
    tblk = jnp.right_shift(tl, SLC_LEN.bit_length() - 1).astype(F32)
    valid = jf <= tblk
    forced = (jf == 0.0) | (jf == tblk) | (jf == tblk - 1.0)
    score = jnp.where(valid, jnp.where(forced, jnp.inf, imp), -jnp.inf)
    sel = jnp.zeros(imp.shape, F32)
    big = float(imp.shape[0])
    for _ in range(n_top):
        mx = jnp.max(score, axis=0, keepdims=True)
        idx = jnp.min(jnp.where(score == mx, jf, big), axis=0, keepdims=True)
        hit = jf == idx
        sel = jnp.where(hit & (mx > -jnp.inf), 1.0, sel)
        score = jnp.where(hit, -jnp.inf, score)
    sel_ref[0, 0] = sel
    cnt = _dot_nt(jnp.ones((SUBLANE, tq), F32), sel)
    flag_ref[0, 0] = (_dot(cnt, pm_ref[...]) > 0.0).astype(jnp.int32)


def _cmpattn(q, kcmp, vcmpt, ov, pm, *, tq, n_top):
    s = q.shape[0]
    g, n16 = kcmp.shape[1], kcmp.shape[2]
    n_slc = ov.shape[0]
    nq = s // tq
    gw = NSA_GROUP * LANE
    return pl.pallas_call(
        functools.partial(_cmpattn_body, tq=tq, n_top=n_top),
        grid=(g, nq),
        in_specs=[
            pl.BlockSpec((tq, gw), lambda a, i: (i, a)),
            pl.BlockSpec((1, 1, n16, LANE), lambda a, i: (0, a, 0, 0)),
            pl.BlockSpec((1, 1, LANE, n16), lambda a, i: (1, a, 0, 0)),
            _const_spec(ov.shape),
            _const_spec(pm.shape),
        ],
        out_specs=[
            pl.BlockSpec((tq, gw), lambda a, i: (i, a)),
            pl.BlockSpec((1, 1, n_slc, tq), lambda a, i: (a, i, 0, 0)),
            pl.BlockSpec((1, 1, SUBLANE, LANE), lambda a, i: (a, i, 0, 0)),
        ],
        out_shape=[
            jax.ShapeDtypeStruct((s, g * gw), F32),
            jax.ShapeDtypeStruct((g, nq, n_slc, tq), F32),
            jax.ShapeDtypeStruct((g, nq, SUBLANE, LANE), jnp.int32),
        ],
        compiler_params=_params("parallel", "arbitrary"),
        name="nsa_cmpattn",
    )(q, kcmp, vcmpt, ov, pm)


def _selattn_body(flag_ref, q_ref, k_ref, vt_ref, sel_ref, o_ref, m_ref, l_ref, acc_ref, *, tq):
    qi = pl.program_id(1)
    q3 = _stack_heads(q_ref[...])
    m_ref[...] = jnp.full_like(m_ref, NEG)
    l_ref[...] = jnp.zeros_like(l_ref)
    acc_ref[...] = jnp.zeros_like(acc_ref)
    half = LANE // 2

    def chunk(c, causal):
        start = pl.multiple_of(c * LANE, LANE)
        k = k_ref[pl.ds(start, LANE), :]
        st = _dot_nt(k, q3)
        rows = []
        for b in range(2):
            srow = sel_ref[0, 0, pl.ds(2 * c + b, 1), :]
            brow = (jnp.concatenate([srow] * NSA_GROUP, axis=1) - 1.0) * (-NEG)
            rows.append(jnp.broadcast_to(brow, (half, brow.shape[1])))
        st = st + jnp.concatenate(rows, axis=0)
        if causal:
            kpos = lax.broadcasted_iota(jnp.int32, st.shape, 0)
            tpos = lax.broadcasted_iota(jnp.int32, st.shape, 1) & (tq - 1)
            st = jnp.where(kpos <= tpos, st, NEG)
        m_old = m_ref[...]
        m_new = jnp.maximum(m_old, jnp.max(st, axis=0, keepdims=True))
        alpha = jnp.exp(m_old - m_new)
        p = jnp.exp(st - m_new)
        l_ref[...] = alpha * l_ref[...] + jnp.sum(p, axis=0, keepdims=True)
        acc_ref[...] = alpha * acc_ref[...] + _dot(vt_ref[0, c], p.astype(BF16))
        m_ref[...] = m_new

    def loop_body(c, carry):
        @pl.when(flag_ref[0, 0, 0, c] > 0)
        def _():
            chunk(c, False)
        return carry

    lax.fori_loop(0, qi, loop_body, 0)
    chunk(qi, True)
    o_ref[...] = _unstack_heads((acc_ref[...] / l_ref[...]).T).astype(o_ref.dtype)


def _selattn(flags, q, ks, vst, sel, *, tq):
    s = q.shape[0]
    g, nq, n_slc, _ = sel.shape
    nchunk = vst.shape[1]
    gw = NSA_GROUP * LANE
    return pl.pallas_call(
        functools.partial(_selattn_body, tq=tq),
        grid=(g, nq),
        in_specs=[
            pl.BlockSpec((1, 1, SUBLANE, LANE), lambda a, i: (a, i, 0, 0), memory_space=pltpu.SMEM),
            pl.BlockSpec((tq, gw), lambda a, i: (i, a)),
            pl.BlockSpec((s, LANE), lambda a, i: (0, a)),
            pl.BlockSpec((1, nchunk, LANE, LANE), lambda a, i: (a, 0, 0, 0)),
            pl.BlockSpec((1, 1, n_slc, tq), lambda a, i: (a, i, 0, 0)),
        ],
        out_specs=pl.BlockSpec((tq, gw), lambda a, i: (i, a)),
        out_shape=jax.ShapeDtypeStruct((s, g * gw), F32),
        scratch_shapes=[pltpu.VMEM((1, gw), F32), pltpu.VMEM((1, gw), F32), pltpu.VMEM((LANE, gw), F32)],
        compiler_params=_params("parallel", "arbitrary"),
        name="nsa_selattn",
    )(flags, q, ks, vst, sel)


def _winattn_body(*refs, tq, nwb):
    q_ref = refs[0]
    k_refs = refs[1:2 + nwb]
    v_refs = refs[2 + nwb:3 + 2 * nwb]
    o_ref = refs[3 + 2 * nwb]
    qi = pl.program_id(1)
    q3 = _stack_heads(q_ref[...])
    k = jnp.concatenate([r[...] for r in k_refs], axis=0)
    v = jnp.concatenate([r[...] for r in v_refs], axis=0)
    s = _dot_nt(q3, k)
    tpos = qi * tq + (lax.broadcasted_iota(jnp.int32, s.shape, 0) & (tq - 1))
    kpos = (qi - nwb) * tq + lax.broadcasted_iota(jnp.int32, s.shape, 1)
    mask = (kpos >= 0) & (kpos <= tpos) & (tpos - kpos < WINDOW)
    s = jnp.where(mask, s, NEG)
    m = jnp.max(s, axis=1, keepdims=True)
    p = jnp.where(mask, jnp.exp(s - m), 0.0)
    l = jnp.sum(p, axis=1, keepdims=True)
    o3 = _dot(p.astype(BF16), v) / l
    o_ref[...] = _unstack_heads(o3).astype(o_ref.dtype)


def _winattn(q, kw, vw, *, tq):
    s = q.shape[0]
    g = kw.shape[1] // LANE
    nwb = WINDOW // tq
    gw = NSA_GROUP * LANE

    def kv_spec(back):
        return pl.BlockSpec((tq, LANE), lambda a, i: (jnp.maximum(i - back, 0), a))

    kv_specs = [kv_spec(nwb - b) for b in range(nwb + 1)]
    return pl.pallas_call(
        functools.partial(_winattn_body, tq=tq, nwb=nwb),
        grid=(g, s // tq),
        in_specs=[pl.BlockSpec((tq, gw), lambda a, i: (i, a))] + kv_specs + kv_specs,
        out_specs=pl.BlockSpec((tq, gw), lambda a, i: (i, a)),
        out_shape=jax.ShapeDtypeStruct((s, g * gw), F32),
        compiler_params=_params("parallel", "parallel"),
        name="nsa_winattn",
    )(q, *([kw] * (nwb + 1)), *([vw] * (nwb + 1)))


def _outproj_body(*refs, mode):
    if mode == "nsa":
        x_ref, oc_ref, os_ref, ow_ref, gl_ref, qx_ref, mk_ref, mv_ref, wmix_ref, wmem_ref, o_ref = refs
        gate = jax.nn.sigmoid(gl_ref[...])
        parts = []
        for hd in range(MIX_HEADS):
            sl = slice(hd * LANE, (hd + 1) * LANE)
            acc = None
            for b, br in enumerate((oc_ref, os_ref, ow_ref)):
                c = b * MIX_HEADS + hd
                term = gate[:, c:c + 1] * br[:, sl]
                acc = term if acc is None else acc + term
            parts.append(acc.astype(BF16))
        ymix = jnp.concatenate(parts, axis=1)
    else:
        x_ref, ymix_ref, qx_ref, mk_ref, mv_ref, wmix_ref, wmem_ref, o_ref = refs
        ymix = ymix_ref[...]
    y = _dot(ymix, wmix_ref[...])
    for hd in range(MEM_HEADS):
        sl = slice(hd * LANE, (hd + 1) * LANE)
        s = _dot_nt(qx_ref[:, sl], mk_ref[:, sl]) * SCALE
        m = jnp.max(s, axis=1, keepdims=True)
        p = jnp.exp(s - m)
        p = p / jnp.sum(p, axis=1, keepdims=True)
        ymem = _dot(p.astype(BF16), mv_ref[:, sl]).astype(BF16)
        y = y + _dot(ymem, wmem_ref[sl, :])
    o_ref[...] = x_ref[...] + y


def _outproj(x, mix_args, qx, mk, mv, wmix, wmem, *, mode, tm):
    s, d = x.shape
    row = lambda w: pl.BlockSpec((tm, w), lambda i: (i, 0))
    mix_specs = [row(a.shape[1]) for a in mix_args]
    return pl.pallas_call(
        functools.partial(_outproj_body, mode=mode),
        grid=(s // tm,),
        in_specs=[row(d)] + mix_specs + [row(qx.shape[1]), _const_spec(mk.shape), _const_spec(mv.shape),
                                         _const_spec(wmix.shape), _const_spec(wmem.shape)],
        out_specs=row(d),
        out_shape=jax.ShapeDtypeStruct((s, d), F32),
        compiler_params=_params("parallel"),
        name="outproj",
    )(x, *mix_args, qx, mk, mv, wmix, wmem)


def _pad_heads_cols(w, n_heads):
    d = w.shape[0]
    w = w.reshape(d, n_heads, HEAD_DIM)
    return jnp.pad(w, ((0, 0), (0, 0), (0, LANE - HEAD_DIM))).reshape(d, n_heads * LANE)


def _pad_heads_rows(w, n_heads):
    d = w.shape[1]
    w = w.reshape(n_heads, HEAD_DIM, d)
    return jnp.pad(w, ((0, 0), (0, LANE - HEAD_DIM), (0, 0))).reshape(n_heads * LANE, d)


def _pad_cols(w, width):
    return jnp.pad(w, ((0, 0), (0, width - w.shape[1])))


def _rope_tables(s):
    half = ROT_DIM // 2
    inv = ROPE_THETA ** (-jnp.arange(half, dtype=F32) / half)
    ang = jnp.arange(s, dtype=F32)[:, None] * inv[None, :]
    cos, sin = jnp.cos(ang), jnp.sin(ang)
    z = lambda n: jnp.zeros((s, n), F32)
    cos_t = jnp.concatenate([cos, cos, jnp.ones((s, LANE - ROT_DIM), F32)], axis=1)
    sa_t = jnp.concatenate([z(half), sin, z(LANE - ROT_DIM)], axis=1)
    sb_t = jnp.concatenate([-sin, z(LANE - half)], axis=1)
    return cos_t, sa_t, sb_t


def _pick(n, candidates):
    for c in candidates:
        if n % c == 0:
            return c
    raise ValueError(f"no tile size for {n}")


def kernel(x, mem, ffn1_norm, ffn1_w_in, ffn1_w_out, mix_norm, mix_w_out, mem_norm, mem_w_kv, fox_w_in,
           fox_b_f, conv_w_in, conv_w, nsa_w_in, nsa_cmp_pos, nsa_cmp_w1, nsa_cmp_w2, ffn2_norm, ffn2_w_in,
           ffn2_w_out, final_norm):
    b, s, d = x.shape
    assert b == 1
    depth = ffn1_norm.shape[0]
    mix_w = MIX_HEADS * HEAD_DIM
    mem_w = MEM_HEADS * HEAD_DIM
    kv_w = NSA_KV_HEADS * HEAD_DIM
    d_ff = ffn1_w_out.shape[1]
    tm_ffn = _pick(s, (1024, 512, 256, 128))
    fc = _pick(d_ff, (256, 128))
    tm = _pick(s, (512, 256, 128))
    tq_fox = _pick(s, (512, 256, 128))
    tq_nsa = 128
    tq_win = _pick(s, (256, 128))
    m_len = mem.shape[1]

    xs = x[0]
    mem_s = mem[0]
    rope_t = None
    for i in range(depth):
        xs = _ffn(xs, ffn1_norm[i], ffn1_w_in[i].astype(BF16), ffn1_w_out[i].astype(BF16), tm=tm_ffn, fc=fc)

        wkv = mem_w_kv[i]
        wkv_p = jnp.concatenate([_pad_heads_cols(wkv[:, :mem_w], MEM_HEADS),
                                 _pad_heads_cols(wkv[:, mem_w:], MEM_HEADS)], axis=1).astype(BF16)
        mk, mv = _proj(mem_s, mem_norm, wkv_p, [(MEM_HEADS * LANE, BF16, False)] * 2, tm=m_len)

        w_out = mix_w_out[i]
        wmem = _pad_heads_rows(w_out[mix_w:], MEM_HEADS).astype(BF16)
        kind, j = i % N_MIXERS, i // N_MIXERS
        if kind == 0:
            w = fox_w_in[j]
            wq, wk, wv = (w[:, a * mix_w:(a + 1) * mix_w] for a in range(3))
            wf = w[:, 3 * mix_w:3 * mix_w + MIX_HEADS]
            wqx = w[:, 3 * mix_w + MIX_HEADS:]
            wp = jnp.concatenate([_pad_heads_cols(wq, MIX_HEADS), _pad_heads_cols(wk, MIX_HEADS),
                                  _pad_heads_cols(wv, MIX_HEADS), _pad_heads_cols(wqx, MEM_HEADS),
                                  _pad_cols(wf, LANE)], axis=1).astype(BF16)
            hw = MIX_HEADS * LANE
            q, k, v, qx, f = _proj(xs, mix_norm[i], wp,
                                   [(hw, BF16, False)] * 3 + [(MEM_HEADS * LANE, BF16, False), (LANE, F32, False)],
                                   tm=tm)
            ft = _fgate(f, _pad_cols(fox_b_f[j].reshape(1, MIX_HEADS), LANE), tk=tq_fox)
            ymix = _fox(q, k, v, ft, tq=tq_fox)
            wmix = _pad_heads_rows(w_out[:mix_w], MIX_HEADS).astype(BF16)
            xs = _outproj(xs, [ymix], qx, mk, mv, wmix, wmem, mode="plain", tm=tm)
        elif kind == 1:
            w = conv_w_in[j]
            wp = jnp.concatenate([w[:, :3 * mix_w], _pad_heads_cols(w[:, 3 * mix_w:], MEM_HEADS)],
                                 axis=1).astype(BF16)
            bg, cg, v, qx = _proj(xs, mix_norm[i], wp,
                                  [(mix_w, F32, False)] * 3 + [(MEM_HEADS * LANE, BF16, False)], tm=tm)
            ymix = _conv(bg, cg, v, conv_w[j], tm=tm)
            xs = _outproj(xs, [ymix], qx, mk, mv, w_out[:mix_w].astype(BF16), wmem, mode="plain", tm=tm)
        else:
            w = nsa_w_in[j]
            off = mix_w
            wq = w[:, :mix_w]
            kvs = []
            for _ in range(6):
                kvs.append(_pad_heads_cols(w[:, off:off + kv_w], NSA_KV_HEADS))
                off += kv_w
            wgl = w[:, off:off + 3 * MIX_HEADS]
            wqx = w[:, off + 3 * MIX_HEADS:]
            wp = jnp.concatenate([_pad_heads_cols(wq, MIX_HEADS)] + kvs +
                                 [_pad_heads_cols(wqx, MEM_HEADS), _pad_cols(wgl, LANE)], axis=1).astype(BF16)
            if rope_t is None:
                rope_t = _rope_tables(s)
            gwid = NSA_KV_HEADS * LANE
            segs = [(MIX_HEADS * LANE, BF16, True)]
            segs += [(gwid, BF16, a % 2 == 0) for a in range(6)]
            segs += [(MEM_HEADS * LANE, BF16, False), (LANE, F32, False)]
            q, kc, vc, ks_, vs_, kw, vw, qx, gl = _proj(xs, mix_norm[i], wp, segs, rope_t, tm=tm)

            n16 = s // CMP_STRIDE
            sub = CMP_STRIDE

            def to_t16(t):
                t = t.reshape(n16, sub, NSA_KV_HEADS, LANE)[..., :HEAD_DIM]
                return t.transpose(2, 0, 1, 3).reshape(NSA_KV_HEADS, n16, sub * HEAD_DIM)

            t16 = jnp.stack([to_t16(kc), to_t16(vc)])
            w1 = nsa_cmp_w1[j].astype(BF16)
            pe8 = jnp.zeros((2, SUBLANE, CMP_LEN * HEAD_DIM), F32).at[:, 0].set(
                nsa_cmp_pos[j].reshape(2, CMP_LEN * HEAD_DIM)).astype(BF16)
            w2d = jnp.pad(nsa_cmp_w2[j], ((0, 0), (0, 0), (0, LANE - HEAD_DIM))).astype(BF16)
            cmp_n, cmp_t = _cmp(t16, w1, pe8, w2d)

            n_slc = s // SLC_LEN
            n_top = min(SLC_TOPN, n_slc)
            jj = jnp.arange(n_slc)[:, None] * SLC_LEN
            nn = jnp.arange(n16)[None, :] * CMP_STRIDE
            ov = ((nn < jj + SLC_LEN) & (nn + CMP_LEN > jj) & (nn + CMP_LEN <= s)).astype(BF16)
            n_chunk = s // LANE
            blocks_per_chunk = LANE // SLC_LEN
            pm = (jnp.arange(n_slc)[:, None] // blocks_per_chunk == jnp.arange(LANE)[None, :]).astype(F32)
            assert n_chunk <= LANE
            oc, sel, flags = _cmpattn(q, cmp_n, cmp_t, ov, pm, tq=tq_nsa, n_top=n_top)

            vst = vs_.reshape(n_chunk, LANE, NSA_KV_HEADS, LANE).transpose(2, 0, 3, 1)
            osel = _selattn(flags, q, ks_, vst, sel, tq=tq_nsa)
            ow = _winattn(q, kw, vw, tq=tq_win)
            wmix = _pad_heads_rows(w_out[:mix_w], MIX_HEADS).astype(BF16)
            xs = _outproj(xs, [oc, osel, ow, gl], qx, mk, mv, wmix, wmem, mode="nsa", tm=tm)

        last = i == depth - 1
        xs = _ffn(xs, ffn2_norm[i], ffn2_w_in[i].astype(BF16), ffn2_w_out[i].astype(BF16),
                  final_norm if last else None, tm=tm_ffn, fc=fc)
    return xs[None]
```

```python
import functools
import math

import jax
import jax.numpy as jnp
from jax import lax
from jax.experimental import pallas as pl
from jax.experimental.pallas import tpu as pltpu

F32 = jnp.float32
BF16 = jnp.bfloat16

HEAD_DIM = 64
MIX_HEADS = 12
MEM_HEADS = 4
N_MIXERS = 3
ROT_DIM = HEAD_DIM // 4
ROPE_THETA = 500000.0
CONV_WIDTH = 3
NSA_KV_HEADS = 4
NSA_GROUP = MIX_HEADS // NSA_KV_HEADS
CMP_LEN = 32
CMP_STRIDE = 16
SLC_LEN = 64
SLC_TOPN = 16
WINDOW = 512
EPS = 1e-6

LANE = 128
SUBLANE = 8
VMEM_LIMIT_BYTES = 56 * 2**20
NEG = -1e30
SCALE = 1.0 / math.sqrt(HEAD_DIM)


def _params(*sem):
    return pltpu.CompilerParams(dimension_semantics=sem, vmem_limit_bytes=VMEM_LIMIT_BYTES)


def _const_spec(shape):
    zeros = (0,) * len(shape)
    return pl.BlockSpec(shape, lambda *_: zeros, pipeline_mode=pl.Buffered(1))


def _rms_bf16(x, g):
    ms = jnp.mean(x * x, axis=-1, keepdims=True)
    return (x * lax.rsqrt(ms + EPS) * g).astype(BF16)


def _dot(a, b):
    return jnp.dot(a, b, preferred_element_type=F32)


def _dot_nt(a, b):
    return lax.dot_general(a, b, (((1,), (1,)), ((), ())), preferred_element_type=F32)


def _ffn_body(x_ref, g_ref, wg_ref, wu_ref, wo_ref, fg_ref, o_ref, hn_ref, acc_ref, *, final_norm):
    j = pl.program_id(1)

    @pl.when(j == 0)
    def _():
        hn_ref[...] = _rms_bf16(x_ref[...], g_ref[...])

    hn = hn_ref[...]
    gate = _dot(hn, wg_ref[...])
    up = _dot(hn, wu_ref[...])
    act = (gate * jax.nn.sigmoid(gate) * up).astype(BF16)
    part = _dot(act, wo_ref[...])

    @pl.when(j == 0)
    def _():
        acc_ref[...] = part

    @pl.when(j > 0)
    def _():
        acc_ref[...] += part

    @pl.when(j == pl.num_programs(1) - 1)
    def _():
        y = x_ref[...] + 0.5 * acc_ref[...]
        if final_norm:
            ms = jnp.mean(y * y, axis=-1, keepdims=True)
            y = y * lax.rsqrt(ms + EPS) * fg_ref[...]
        o_ref[...] = y


def _ffn(x, g, w_in, w_out, final_g=None, *, tm, fc):
    s, d = x.shape
    f = w_out.shape[0]
    nf = f // fc
    fg = jnp.ones((1, d), F32) if final_g is None else final_g.reshape(1, d)
    return pl.pallas_call(
        functools.partial(_ffn_body, final_norm=final_g is not None),
        grid=(s // tm, nf),
        in_specs=[
            pl.BlockSpec((tm, d), lambda i, j: (i, 0)),
            pl.BlockSpec((1, d), lambda i, j: (0, 0)),
            pl.BlockSpec((d, fc), lambda i, j: (0, j)),
            pl.BlockSpec((d, fc), lambda i, j: (0, nf + j)),
            pl.BlockSpec((fc, d), lambda i, j: (j, 0)),
            pl.BlockSpec((1, d), lambda i, j: (0, 0)),
        ],
        out_specs=pl.BlockSpec((tm, d), lambda i, j: (i, 0)),
        out_shape=jax.ShapeDtypeStruct((s, d), F32),
        scratch_shapes=[pltpu.VMEM((tm, d), BF16), pltpu.VMEM((tm, d), F32)],
        compiler_params=_params("parallel", "arbitrary"),
        name="ffn",
    )(x, g.reshape(1, d), w_in, w_in, w_out, fg)


def _proj_body(*refs, segs, has_rope, chunk):
    n_in = 6 if has_rope else 3
    x_ref, g_ref, w_ref = refs[:3]
    outs = refs[n_in:]
    hn = _rms_bf16(x_ref[...], g_ref[...])
    if has_rope:
        cos_t, sa_t, sb_t = (r[...] for r in refs[3:6])
    col = 0
    for o_ref, (width, rope) in zip(outs, segs):
        for c0 in range(0, width, chunk):
            cw = min(chunk, width - c0)
            y = _dot(hn, w_ref[:, col + c0:col + c0 + cw])
            if rope:
                rep = cw // LANE
                cos_c, sa_c, sb_c = (jnp.concatenate([t] * rep, axis=1) if rep > 1 else t
                                     for t in (cos_t, sa_t, sb_t))
                half = ROT_DIM // 2
                y = y * cos_c + pltpu.roll(y, half, axis=1) * sa_c + pltpu.roll(y, cw - half, axis=1) * sb_c
            o_ref[:, c0:c0 + cw] = y.astype(o_ref.dtype)
        col += width


def _proj(x, g, w, segs, rope_tables=None, *, tm, chunk=256):
    s, d = x.shape
    n = w.shape[1]
    assert n == sum(sg[0] for sg in segs)
    has_rope = rope_tables is not None
    in_specs = [
        pl.BlockSpec((tm, d), lambda i: (i, 0)),
        pl.BlockSpec((1, d), lambda i: (0, 0)),
        _const_spec((d, n)),
    ]
    args = [x, g.reshape(1, d), w]
    if has_rope:
        in_specs += [pl.BlockSpec((tm, LANE), lambda i: (i, 0))] * 3
        args += list(rope_tables)
    return pl.pallas_call(
        functools.partial(_proj_body, segs=tuple((sg[0], sg[2]) for sg in segs), has_rope=has_rope,
                          chunk=chunk),
        grid=(s // tm,),
        in_specs=in_specs,
        out_specs=[pl.BlockSpec((tm, sg[0]), lambda i: (i, 0)) for sg in segs],
        out_shape=[jax.ShapeDtypeStruct((s, sg[0]), sg[1]) for sg in segs],
        compiler_params=_params("parallel"),
        name="proj",
    )(*args)


def _fgate_body(f_ref, b_ref, o_ref, carry_ref):
    i = pl.program_id(0)

    @pl.when(i == 0)
    def _():
        carry_ref[...] = jnp.zeros_like(carry_ref)

    z = f_ref[...] + b_ref[...]
    x = jnp.minimum(z, 0.0) - jnp.log(1.0 + jnp.exp(-jnp.abs(z)))
    tm = x.shape[0]
    row = lax.broadcasted_iota(jnp.int32, x.shape, 0)
    sh = 1
    while sh < tm:
        x = x + jnp.where(row >= sh, pltpu.roll(x, sh, axis=0), 0.0)
        sh *= 2
    x = x + carry_ref[...]
    carry_ref[...] = x[tm - 1:tm, :]
    o_ref[0] = x.T[:2 * SUBLANE, :]


def _fgate(f, b, *, tk):
    s = f.shape[0]
    return pl.pallas_call(
        _fgate_body,
        grid=(s // tk,),
        in_specs=[pl.BlockSpec((tk, LANE), lambda i: (i, 0)), pl.BlockSpec((1, LANE), lambda i: (0, 0))],
        out_specs=pl.BlockSpec((1, 2 * SUBLANE, tk), lambda i: (i, 0, 0)),
        out_shape=jax.ShapeDtypeStruct((s // tk, 2 * SUBLANE, tk), F32),
        scratch_shapes=[pltpu.VMEM((1, LANE), F32)],
        compiler_params=_params("arbitrary"),
        name="fgate",
    )(f, b)


UNDERFLOW = 110.0


def _fox_body(q_ref, k_ref, v_ref, ft_ref, fend_ref, o_ref, m_ref, l_ref, acc_ref, kmax2_ref, *, tq):
    h = pl.program_id(0)
    qi = pl.program_id(1)
    q = q_ref[...] * SCALE
    f_ref_row = ft_ref[qi, pl.ds(h, 1), :]
    f0 = f_ref_row[:, 0:1]
    m_ref[...] = jnp.full_like(m_ref, NEG)
    l_ref[...] = jnp.zeros_like(l_ref)
    acc_ref[...] = jnp.zeros_like(acc_ref)

    @pl.when(qi == 0)
    def _():
        def kmax_body(c, mx):
            kf = k_ref[pl.ds(pl.multiple_of(c * tq, tq), tq), :].astype(F32)
            return jnp.maximum(mx, jnp.sum(kf * kf, axis=1, keepdims=True))

        mx = lax.fori_loop(0, k_ref.shape[0] // tq, kmax_body, jnp.zeros((tq, 1), F32))
        kmax2_ref[...] = jnp.max(mx, axis=0, keepdims=True)

    qf = q.astype(F32)
    qmax2 = jnp.max(jnp.sum(qf * qf, axis=1, keepdims=True), axis=0, keepdims=True)
    bound = f0 + (UNDERFLOW + 2.0 * jnp.sqrt(qmax2 * kmax2_ref[...]))
    fend = fend_ref[pl.ds(h, 1), :]
    blk = lax.broadcasted_iota(jnp.int32, fend.shape, 1)
    kb_lo = jnp.sum(jnp.where((fend > bound) & (blk < qi), 1, 0))

    def step(kb, causal):
        start = pl.multiple_of(kb * tq, tq)
        k = k_ref[pl.ds(start, tq), :]
        v = v_ref[pl.ds(start, tq), :]
        s = _dot_nt(q, k) + (f0 - ft_ref[kb, pl.ds(h, 1), :])
        if causal:
            r = lax.broadcasted_iota(jnp.int32, s.shape, 0)
            c = lax.broadcasted_iota(jnp.int32, s.shape, 1)
            s = jnp.where(c <= r, s, NEG)
        m_old = m_ref[...]
        m_new = jnp.maximum(m_old, jnp.max(s, axis=1, keepdims=True))
        alpha = jnp.exp(m_old - m_new)
        p = jnp.exp(s - m_new)
        l_ref[...] = alpha * l_ref[...] + jnp.sum(p, axis=1, keepdims=True)
        acc_ref[...] = alpha * acc_ref[...] + _dot(p.astype(BF16), v)
        m_ref[...] = m_new

    def loop_body(kb, carry):
        step(kb, False)
        return carry

    lax.fori_loop(kb_lo, qi, loop_body, 0)
    step(qi, True)
    o_ref[...] = (acc_ref[...] / l_ref[...]).astype(o_ref.dtype)


def _fox(q, k, v, ft, *, tq):
    s = q.shape[0]
    nh = q.shape[1] // LANE
    nq = s // tq
    assert nq <= LANE
    fend = jnp.pad(ft[:, :, tq - 1].T, ((0, 0), (0, LANE - nq)))
    return pl.pallas_call(
        functools.partial(_fox_body, tq=tq),
        grid=(nh, nq),
        in_specs=[
            pl.BlockSpec((tq, LANE), lambda h, i: (i, h)),
            pl.BlockSpec((s, LANE), lambda h, i: (0, h)),
            pl.BlockSpec((s, LANE), lambda h, i: (0, h)),
            _const_spec(ft.shape),
            _const_spec(fend.shape),
        ],
        out_specs=pl.BlockSpec((tq, LANE), lambda h, i: (i, h)),
        out_shape=jax.ShapeDtypeStruct((s, nh * LANE), BF16),
        scratch_shapes=[pltpu.VMEM((tq, 1), F32), pltpu.VMEM((tq, 1), F32), pltpu.VMEM((tq, LANE), F32),
                        pltpu.VMEM((1, 1), F32)],
        compiler_params=_params("arbitrary", "arbitrary"),
        name="fox",
    )(q, k, v, ft, fend)


def _conv_body(bg_ref, cg_ref, v_ref, cgh_ref, vh_ref, w_ref, o_ref):
    i = pl.program_id(0)
    u = cg_ref[...] * v_ref[...]
    halo = jnp.where(i > 0, cgh_ref[...] * vh_ref[...], 0.0)
    ext = jnp.concatenate([halo, u], axis=0)
    u1 = pltpu.roll(ext, 1, axis=0)[SUBLANE:, :]
    u2 = pltpu.roll(ext, 2, axis=0)[SUBLANE:, :]
    w = w_ref[...]
    y = w[0:1, :] * u2 + w[1:2, :] * u1 + w[2:3, :] * u
    o_ref[...] = (bg_ref[...] * y).astype(o_ref.dtype)


def _conv(bg, cg, v, w, *, tm):
    s, c = bg.shape
    w8 = jnp.zeros((SUBLANE, c), F32).at[:CONV_WIDTH].set(w)
    per = tm // SUBLANE
    main = pl.BlockSpec((tm, c), lambda i: (i, 0))
    halo = pl.BlockSpec((SUBLANE, c), lambda i: (jnp.maximum(i * per - 1, 0), 0))
    return pl.pallas_call(
        _conv_body,
        grid=(s // tm,),
        in_specs=[main, main, main, halo, halo, pl.BlockSpec((SUBLANE, c), lambda i: (0, 0))],
        out_specs=main,
        out_shape=jax.ShapeDtypeStruct((s, c), BF16),
        compiler_params=_params("parallel"),
        name="conv",
    )(bg, cg, v, cg, v, w8)


def _cmp_body(t_ref, w1_ref, pe_ref, w2_ref, on_ref, ot_ref):
    t = t_ref[0, 0]
    n16 = t.shape[0]
    half = w1_ref.shape[1] // 2
    a = _dot(t, w1_ref[0, :half, :])
    b = _dot(t, w1_ref[0, half:, :])
    pec = _dot(pe_ref[0], w1_ref[0])[0:1, :]
    pre = a + pltpu.roll(b, n16 - 1, axis=0) + pec
    hid = (pre * jax.nn.sigmoid(pre)).astype(BF16)
    out = _dot(hid, w2_ref[0])
    on_ref[0, 0] = out.astype(on_ref.dtype)
    ot_ref[0, 0] = out.T.astype(ot_ref.dtype)


def _cmp(t16, w1, pe8, w2d):
    _, g, n16, kk = t16.shape
    return pl.pallas_call(
        _cmp_body,
        grid=(2, g),
        in_specs=[
            pl.BlockSpec((1, 1, n16, kk), lambda a, b: (a, b, 0, 0)),
            pl.BlockSpec((1, 2 * kk, HEAD_DIM), lambda a, b: (a, 0, 0)),
            pl.BlockSpec((1, SUBLANE, 2 * kk), lambda a, b: (a, 0, 0)),
            pl.BlockSpec((1, HEAD_DIM, LANE), lambda a, b: (a, 0, 0)),
        ],
        out_specs=[
            pl.BlockSpec((1, 1, n16, LANE), lambda a, b: (a, b, 0, 0)),
            pl.BlockSpec((1, 1, LANE, n16), lambda a, b: (a, b, 0, 0)),
        ],
        out_shape=[
            jax.ShapeDtypeStruct((2, g, n16, LANE), BF16),
            jax.ShapeDtypeStruct((2, g, LANE, n16), BF16),
        ],
        compiler_params=_params("parallel", "parallel"),
        name="nsa_cmp",
    )(t16, w1, pe8, w2d)


def _stack_heads(q_blk):
    return jnp.concatenate([q_blk[:, j * LANE:(j + 1) * LANE] for j in range(NSA_GROUP)], axis=0) * SCALE


def _unstack_heads(o3):
    tq = o3.shape[0] // NSA_GROUP
    return jnp.concatenate([o3[j * tq:(j + 1) * tq, :] for j in range(NSA_GROUP)], axis=1)


def _cmpattn_body(q_ref, kc_ref, vct_ref, ov_ref, oc_ref, sel_ref, *, tq, n_top):
    qi = pl.program_id(1)
    t0 = qi * tq
    q3 = _stack_heads(q_ref[...])
    kc = kc_ref[0, 0]
    n16 = kc.shape[0]
    st = _dot_nt(kc, q3)
    n_idx = lax.broadcasted_iota(jnp.int32, st.shape, 0)
    lane = lax.broadcasted_iota(jnp.int32, st.shape, 1)
    tpos = t0 + (lane & (tq - 1))
    mask = n_idx * CMP_STRIDE + (CMP_LEN - 1) <= tpos
    st = jnp.where(mask, st, NEG)
    m = jnp.max(st, axis=0, keepdims=True)
    p = jnp.where(mask, jnp.exp(st - m), 0.0)
    l = jnp.sum(p, axis=0, keepdims=True)
    inv_l = 1.0 / jnp.maximum(l, jnp.finfo(F32).tiny)
    pn = p * inv_l
    oct_ = _dot(vct_ref[0, 0], p.astype(BF16)) * inv_l
    oc_ref[...] = _unstack_heads(oct_.T).astype(oc_ref.dtype)

    psum = pn[:, 0:tq]
    for j in range(1, NSA_GROUP):
        psum = psum + pn[:, j * tq:(j + 1) * tq]
    p_hi = psum.astype(BF16)
    p_lo = (psum - p_hi.astype(F32)).astype(BF16)
    ov = ov_ref[...]
    imp = _dot(ov, p_hi) + _dot(ov, p_lo)

    jf = lax.broadcasted_iota(jnp.int32, imp.shape, 0).astype(F32)
    tl = t0 + lax.broadcasted_iota(jnp.int32, imp.shape, 1)
    tblk = jnp.right_shift(tl, SLC_LEN.bit_length() - 1).astype(F32)
    valid = jf <= tblk
    forced = (jf == 0.0) | (jf == tblk) | (jf == tblk - 1.0)
    score = jnp.where(valid, jnp.where(forced, jnp.inf, imp), -jnp.inf)
    sel = jnp.zeros(imp.shape, F32)
    big = float(imp.shape[0])
    for _ in range(n_top):
        mx = jnp.max(score, axis=0, keepdims=True)
        idx = jnp.min(jnp.where(score == mx, jf, big), axis=0, keepdims=True)
        hit = jf == idx
        sel = jnp.where(hit & (mx > -jnp.inf), 1.0, sel)
        score = jnp.where(hit, -jnp.inf, score)
    sel_ref[0, 0] = sel


def _cmpattn(q, kcmp, vcmpt, ov, *, tq, n_top):
    s = q.shape[0]
    g, n16 = kcmp.shape[1], kcmp.shape[2]
    n_slc = ov.shape[0]
    nq = s // tq
    gw = NSA_GROUP * LANE
    return pl.pallas_call(
        functools.partial(_cmpattn_body, tq=tq, n_top=n_top),
        grid=(g, nq),
        in_specs=[
            pl.BlockSpec((tq, gw), lambda a, i: (i, a)),
            pl.BlockSpec((1, 1, n16, LANE), lambda a, i: (0, a, 0, 0)),
            pl.BlockSpec((1, 1, LANE, n16), lambda a, i: (1, a, 0, 0)),
            _const_spec(ov.shape),
        ],
        out_specs=[
            pl.BlockSpec((tq, gw), lambda a, i: (i, a)),
            pl.BlockSpec((1, 1, n_slc, tq), lambda a, i: (a, i, 0, 0)),
        ],
        out_shape=[
            jax.ShapeDtypeStruct((s, g * gw), F32),
            jax.ShapeDtypeStruct((g, nq, n_slc, tq), F32),
        ],
        compiler_params=_params("parallel", "arbitrary"),
        name="nsa_cmpattn",
    )(q, kcmp, vcmpt, ov)


def _selattn_body(q_ref, k_ref, vt_ref, sel_ref, o_ref, m_ref, l_ref, acc_ref, *, tq, kc, ng):
    qi = pl.program_id(1)
    gw = NSA_GROUP * LANE
    q3s = [_stack_heads(q_ref[:, b * gw:(b + 1) * gw]) for b in range(ng)]
    m_ref[...] = jnp.full_like(m_ref, NEG)
    l_ref[...] = jnp.zeros_like(l_ref)
    acc_ref[...] = jnp.zeros_like(acc_ref)
    bpc = kc // SLC_LEN
    n_full = (qi * tq) // kc

    def chunk(c, causal):
        start = pl.multiple_of(c * kc, kc)
        for b in range(ng):
            k = k_ref[pl.ds(start, kc), b * LANE:(b + 1) * LANE]
            st = _dot_nt(k, q3s[b])
            srows = sel_ref[b, 0, pl.ds(pl.multiple_of(c * bpc, bpc), bpc), :]
            pen = (jnp.concatenate([srows] * NSA_GROUP, axis=1) - 1.0) * (-NEG)
            st = st + jnp.concatenate(
                [jnp.broadcast_to(pen[r:r + 1, :], (SLC_LEN, gw)) for r in range(bpc)], axis=0)
            if causal:
                kpos = c * kc + lax.broadcasted_iota(jnp.int32, st.shape, 0)
                tpos = qi * tq + (lax.broadcasted_iota(jnp.int32, st.shape, 1) & (tq - 1))
                st = jnp.where(kpos <= tpos, st, NEG)
            m_old = m_ref[b]
            m_new = jnp.maximum(m_old, jnp.max(st, axis=0, keepdims=True))
            alpha = jnp.exp(m_old - m_new)
            p = jnp.exp(st - m_new)
            l_ref[b] = alpha * l_ref[b] + jnp.sum(p, axis=0, keepdims=True)
            acc_ref[b] = alpha * acc_ref[b] + _dot(vt_ref[b, c], p.astype(BF16))
            m_ref[b] = m_new

    def loop_body(c, carry):
        chunk(c, False)
        return carry

    lax.fori_loop(0, n_full, loop_body, 0)
    chunk(n_full, True)
    for b in range(ng):
        o_ref[:, b * gw:(b + 1) * gw] = _unstack_heads((acc_ref[b] / l_ref[b]).T).astype(o_ref.dtype)


def _selattn(q, ks, vst, sel, *, tq, ng):
    s = q.shape[0]
    g, nq, n_slc, _ = sel.shape
    _, nchunk, _, kc = vst.shape
    gw = NSA_GROUP * LANE
    return pl.pallas_call(
        functools.partial(_selattn_body, tq=tq, kc=kc, ng=ng),
        grid=(g // ng, nq),
        in_specs=[
            pl.BlockSpec((tq, ng * gw), lambda a, i: (i, a)),
            pl.BlockSpec((s, ng * LANE), lambda a, i: (0, a)),
            pl.BlockSpec((ng, nchunk, LANE, kc), lambda a, i: (a, 0, 0, 0)),
            pl.BlockSpec((ng, 1, n_slc, tq), lambda a, i: (a, i, 0, 0)),
        ],
        out_specs=pl.BlockSpec((tq, ng * gw), lambda a, i: (i, a)),
        out_shape=jax.ShapeDtypeStruct((s, g * gw), F32),
        scratch_shapes=[pltpu.VMEM((ng, 1, gw), F32), pltpu.VMEM((ng, 1, gw), F32),
                        pltpu.VMEM((ng, LANE, gw), F32)],
        compiler_params=_params("parallel", "arbitrary"),
        name="nsa_selattn",
    )(q, ks, vst, sel)


def _winattn_body(*refs, tq, nwb):
    q_ref = refs[0]
    k_refs = refs[1:2 + nwb]
    v_refs = refs[2 + nwb:3 + 2 * nwb]
    o_ref = refs[3 + 2 * nwb]
    qi = pl.program_id(1)
    q3 = _stack_heads(q_ref[...])
    k = jnp.concatenate([r[...] for r in k_refs], axis=0)
    v = jnp.concatenate([r[...] for r in v_refs], axis=0)
    s = _dot_nt(q3, k)
    tpos = qi * tq + (lax.broadcasted_iota(jnp.int32, s.shape, 0) & (tq - 1))
    kpos = (qi - nwb) * tq + lax.broadcasted_iota(jnp.int32, s.shape, 1)
    mask = (kpos >= 0) & (kpos <= tpos) & (tpos - kpos < WINDOW)
    s = jnp.where(mask, s, NEG)
    m = jnp.max(s, axis=1, keepdims=True)
    p = jnp.where(mask, jnp.exp(s - m), 0.0)
    l = jnp.sum(p, axis=1, keepdims=True)
    o3 = _dot(p.astype(BF16), v) / l
    o_ref[...] = _unstack_heads(o3).astype(o_ref.dtype)


def _winattn(q, kw, vw, *, tq):
    s = q.shape[0]
    g = kw.shape[1] // LANE
    nwb = WINDOW // tq
    gw = NSA_GROUP * LANE

    def kv_spec(back):
        return pl.BlockSpec((tq, LANE), lambda a, i: (jnp.maximum(i - back, 0), a))

    kv_specs = [kv_spec(nwb - b) for b in range(nwb + 1)]
    return pl.pallas_call(
        functools.partial(_winattn_body, tq=tq, nwb=nwb),
        grid=(g, s // tq),
        in_specs=[pl.BlockSpec((tq, gw), lambda a, i: (i, a))] + kv_specs + kv_specs,
        out_specs=pl.BlockSpec((tq, gw), lambda a, i: (i, a)),
        out_shape=jax.ShapeDtypeStruct((s, g * gw), F32),
        compiler_params=_params("parallel", "parallel"),
        name="nsa_winattn",
    )(q, *([kw] * (nwb + 1)), *([vw] * (nwb + 1)))


def _outproj_body(*refs, mode):
    if mode == "nsa":
        x_ref, oc_ref, os_ref, ow_ref, gl_ref, qx_ref, mk_ref, mv_ref, wmix_ref, wmem_ref, o_ref = refs
        gate = jax.nn.sigmoid(gl_ref[...])
        parts = []
        for hd in range(MIX_HEADS):
            sl = slice(hd * LANE, (hd + 1) * LANE)
            acc = None
            for b, br in enumerate((oc_ref, os_ref, ow_ref)):
                c = b * MIX_HEADS + hd
                term = gate[:, c:c + 1] * br[:, sl]
                acc = term if acc is None else acc + term
            parts.append(acc.astype(BF16))
        ymix = jnp.concatenate(parts, axis=1)
    else:
        x_ref, ymix_ref, qx_ref, mk_ref, mv_ref, wmix_ref, wmem_ref, o_ref = refs
        ymix = ymix_ref[...]
    y = _dot(ymix, wmix_ref[...])
    for hd in range(MEM_HEADS):
        sl = slice(hd * LANE, (hd + 1) * LANE)
        s = _dot_nt(qx_ref[:, sl], mk_ref[:, sl]) * SCALE
        m = jnp.max(s, axis=1, keepdims=True)
        p = jnp.exp(s - m)
        p = p / jnp.sum(p, axis=1, keepdims=True)
        ymem = _dot(p.astype(BF16), mv_ref[:, sl]).astype(BF16)
        y = y + _dot(ymem, wmem_ref[sl, :])
    o_ref[...] = x_ref[...] + y


def _outproj(x, mix_args, qx, mk, mv, wmix, wmem, *, mode, tm):
    s, d = x.shape
    row = lambda w: pl.BlockSpec((tm, w), lambda i: (i, 0))
    mix_specs = [row(a.shape[1]) for a in mix_args]
    return pl.pallas_call(
        functools.partial(_outproj_body, mode=mode),
        grid=(s // tm,),
        in_specs=[row(d)] + mix_specs + [row(qx.shape[1]), _const_spec(mk.shape), _const_spec(mv.shape),
                                         _const_spec(wmix.shape), _const_spec(wmem.shape)],
        out_specs=row(d),
        out_shape=jax.ShapeDtypeStruct((s, d), F32),
        compiler_params=_params("parallel"),
        name="outproj",
    )(x, *mix_args, qx, mk, mv, wmix, wmem)


def _pad_heads_cols(w, n_heads):
    d = w.shape[0]
    w = w.reshape(d, n_heads, HEAD_DIM)
    return jnp.pad(w, ((0, 0), (0, 0), (0, LANE - HEAD_DIM))).reshape(d, n_heads * LANE)


def _pad_heads_rows(w, n_heads):
    d = w.shape[1]
    w = w.reshape(n_heads, HEAD_DIM, d)
    return jnp.pad(w, ((0, 0), (0, LANE - HEAD_DIM), (0, 0))).reshape(n_heads * LANE, d)


def _pad_cols(w, width):
    return jnp.pad(w, ((0, 0), (0, width - w.shape[1])))


def _rope_tables(s):
    half = ROT_DIM // 2
    inv = ROPE_THETA ** (-jnp.arange(half, dtype=F32) / half)
    ang = jnp.arange(s, dtype=F32)[:, None] * inv[None, :]
    cos, sin = jnp.cos(ang), jnp.sin(ang)
    z = lambda n: jnp.zeros((s, n), F32)
    cos_t = jnp.concatenate([cos, cos, jnp.ones((s, LANE - ROT_DIM), F32)], axis=1)
    sa_t = jnp.concatenate([z(half), sin, z(LANE - ROT_DIM)], axis=1)
    sb_t = jnp.concatenate([-sin, z(LANE - half)], axis=1)
    return cos_t, sa_t, sb_t


def _pick(n, candidates):
    for c in candidates:
        if n % c == 0:
            return c
    raise ValueError(f"no tile size for {n}")


def kernel(x, mem, ffn1_norm, ffn1_w_in, ffn1_w_out, mix_norm, mix_w_out, mem_norm, mem_w_kv, fox_w_in,
           fox_b_f, conv_w_in, conv_w, nsa_w_in, nsa_cmp_pos, nsa_cmp_w1, nsa_cmp_w2, ffn2_norm, ffn2_w_in,
           ffn2_w_out, final_norm):
    b, s, d = x.shape
    assert b == 1
    depth = ffn1_norm.shape[0]
    mix_w = MIX_HEADS * HEAD_DIM
    mem_w = MEM_HEADS * HEAD_DIM
    kv_w = NSA_KV_HEADS * HEAD_DIM
    d_ff = ffn1_w_out.shape[1]
    tm_ffn = _pick(s, (1024, 512, 256, 128))
    fc = _pick(d_ff, (256, 128))
    tm = _pick(s, (512, 256, 128))
    tq_fox = _pick(s, (512, 256, 128))
    tq_nsa = 128
    kc_sel = _pick(s, (512, 256, 128))
    tq_win = _pick(s, (256, 128))
    m_len = mem.shape[1]

    xs = x[0]
    mem_s = mem[0]
    rope_t = None
    for i in range(depth):
        xs = _ffn(xs, ffn1_norm[i], ffn1_w_in[i].astype(BF16), ffn1_w_out[i].astype(BF16), tm=tm_ffn, fc=fc)

        wkv = mem_w_kv[i]
        wkv_p = jnp.concatenate([_pad_heads_cols(wkv[:, :mem_w], MEM_HEADS),
                                 _pad_heads_cols(wkv[:, mem_w:], MEM_HEADS)], axis=1).astype(BF16)
        mk, mv = _proj(mem_s, mem_norm, wkv_p, [(MEM_HEADS * LANE, BF16, False)] * 2, tm=m_len)

        w_out = mix_w_out[i]
        wmem = _pad_heads_rows(w_out[mix_w:], MEM_HEADS).astype(BF16)
        kind, j = i % N_MIXERS, i // N_MIXERS
        if kind == 0:
            w = fox_w_in[j]
            wq, wk, wv = (w[:, a * mix_w:(a + 1) * mix_w] for a in range(3))
            wf = w[:, 3 * mix_w:3 * mix_w + MIX_HEADS]
            wqx = w[:, 3 * mix_w + MIX_HEADS:]
            wp = jnp.concatenate([_pad_heads_cols(wq, MIX_HEADS), _pad_heads_cols(wk, MIX_HEADS),
                                  _pad_heads_cols(wv, MIX_HEADS), _pad_heads_cols(wqx, MEM_HEADS),
                                  _pad_cols(wf, LANE)], axis=1).astype(BF16)
            hw = MIX_HEADS * LANE
            q, k, v, qx, f = _proj(xs, mix_norm[i], wp,
                                   [(hw, BF16, False)] * 3 + [(MEM_HEADS * LANE, BF16, False), (LANE, F32, False)],
                                   tm=tm)
            ft = _fgate(f, _pad_cols(fox_b_f[j].reshape(1, MIX_HEADS), LANE), tk=tq_fox)
            ymix = _fox(q, k, v, ft, tq=tq_fox)
            wmix = _pad_heads_rows(w_out[:mix_w], MIX_HEADS).astype(BF16)
            xs = _outproj(xs, [ymix], qx, mk, mv, wmix, wmem, mode="plain", tm=tm)
        elif kind == 1:
            w = conv_w_in[j]
            wp = jnp.concatenate([w[:, :3 * mix_w], _pad_heads_cols(w[:, 3 * mix_w:], MEM_HEADS)],
                                 axis=1).astype(BF16)
            bg, cg, v, qx = _proj(xs, mix_norm[i], wp,
                                  [(mix_w, F32, False)] * 3 + [(MEM_HEADS * LANE, BF16, False)], tm=tm)
            ymix = _conv(bg, cg, v, conv_w[j], tm=tm)
            xs = _outproj(xs, [ymix], qx, mk, mv, w_out[:mix_w].astype(BF16), wmem, mode="plain", tm=tm)
        else:
            w = nsa_w_in[j]
            off = mix_w
            wq = w[:, :mix_w]
            kvs = []
            for _ in range(6):
                kvs.append(_pad_heads_cols(w[:, off:off + kv_w], NSA_KV_HEADS))
                off += kv_w
            wgl = w[:, off:off + 3 * MIX_HEADS]
            wqx = w[:, off + 3 * MIX_HEADS:]
            wp = jnp.concatenate([_pad_heads_cols(wq, MIX_HEADS)] + kvs +
                                 [_pad_heads_cols(wqx, MEM_HEADS), _pad_cols(wgl, LANE)], axis=1).astype(BF16)
            if rope_t is None:
                rope_t = _rope_tables(s)
            gwid = NSA_KV_HEADS * LANE
            segs = [(MIX_HEADS * LANE, BF16, True)]
            segs += [(gwid, BF16, a % 2 == 0) for a in range(6)]
            segs += [(MEM_HEADS * LANE, BF16, False), (LANE, F32, False)]
            q, kc, vc, ks_, vs_, kw, vw, qx, gl = _proj(xs, mix_norm[i], wp, segs, rope_t, tm=tm)

            n16 = s // CMP_STRIDE
            sub = CMP_STRIDE

            def to_t16(t):
                t = t.reshape(n16, sub, NSA_KV_HEADS, LANE)[..., :HEAD_DIM]
                return t.transpose(2, 0, 1, 3).reshape(NSA_KV_HEADS, n16, sub * HEAD_DIM)

            t16 = jnp.stack([to_t16(kc), to_t16(vc)])
            w1 = nsa_cmp_w1[j].astype(BF16)
            pe8 = jnp.zeros((2, SUBLANE, CMP_LEN * HEAD_DIM), F32).at[:, 0].set(
                nsa_cmp_pos[j].reshape(2, CMP_LEN * HEAD_DIM)).astype(BF16)
            w2d = jnp.pad(nsa_cmp_w2[j], ((0, 0), (0, 0), (0, LANE - HEAD_DIM))).astype(BF16)
            cmp_n, cmp_t = _cmp(t16, w1, pe8, w2d)

            n_slc = s // SLC_LEN
            n_top = min(SLC_TOPN, n_slc)
            jj = jnp.arange(n_slc)[:, None] * SLC_LEN
            nn = jnp.arange(n16)[None, :] * CMP_STRIDE
            ov = ((nn < jj + SLC_LEN) & (nn + CMP_LEN > jj) & (nn + CMP_LEN <= s)).astype(BF16)
            oc, sel = _cmpattn(q, cmp_n, cmp_t, ov, tq=tq_nsa, n_top=n_top)

            vst = vs_.reshape(s // kc_sel, kc_sel, NSA_KV_HEADS, LANE).transpose(2, 0, 3, 1)
            osel = _selattn(q, ks_, vst, sel, tq=tq_nsa, ng=2)
            ow = _winattn(q, kw, vw, tq=tq_win)
            wmix = _pad_heads_rows(w_out[:mix_w], MIX_HEADS).astype(BF16)
            xs = _outproj(xs, [oc, osel, ow, gl], qx, mk, mv, wmix, wmem, mode="nsa", tm=tm)

        last = i == depth - 1
        xs = _ffn(xs, ffn2_norm[i], ffn2_w_in[i].astype(BF16), ffn2_w_out[i].astype(BF16),
                  final_norm if last else None, tm=tm_ffn, fc=fc)
    return xs[None]
```

```python
import functools
import math

import jax
import jax.numpy as jnp
from jax import lax
from jax.experimental import pallas as pl
from jax.experimental.pallas import tpu as pltpu

F32 = jnp.float32
BF16 = jnp.bfloat16

HEAD_DIM = 64
MIX_HEADS = 12
MEM_HEADS = 4
N_MIXERS = 3
ROT_DIM = HEAD_DIM // 4
ROPE_THETA = 500000.0
CONV_WIDTH = 3
NSA_KV_HEADS = 4
NSA_GROUP = MIX_HEADS // NSA_KV_HEADS
CMP_LEN = 32
CMP_STRIDE = 16
SLC_LEN = 64
SLC_TOPN = 16
WINDOW = 512
EPS = 1e-6

LANE = 128
SUBLANE = 8
VMEM_LIMIT_BYTES = 56 * 2**20
NEG = -1e30
SCALE = 1.0 / math.sqrt(HEAD_DIM)


def _params(*sem):
    return pltpu.CompilerParams(dimension_semantics=sem, vmem_limit_bytes=VMEM_LIMIT_BYTES)


def _const_spec(shape):
    zeros = (0,) * len(shape)
    return pl.BlockSpec(shape, lambda *_: zeros, pipeline_mode=pl.Buffered(1))


def _rms_bf16(x, g):
    ms = jnp.mean(x * x, axis=-1, keepdims=True)
    return (x * lax.rsqrt(ms + EPS) * g).astype(BF16)


def _dot(a, b):
    return jnp.dot(a, b, preferred_element_type=F32)


def _dot_nt(a, b):
    return lax.dot_general(a, b, (((1,), (1,)), ((), ())), preferred_element_type=F32)


def _ffn_body(x_ref, g_ref, wg_ref, wu_ref, wo_ref, fg_ref, o_ref, hn_ref, acc_ref, *, final_norm):
    j = pl.program_id(1)

    @pl.when(j == 0)
    def _():
        hn_ref[...] = _rms_bf16(x_ref[...], g_ref[...])

    hn = hn_ref[...]
    gate = _dot(hn, wg_ref[...])
    up = _dot(hn, wu_ref[...])
    act = (gate * jax.nn.sigmoid(gate) * up).astype(BF16)
    part = _dot(act, wo_ref[...])

    @pl.when(j == 0)
    def _():
        acc_ref[...] = part

    @pl.when(j > 0)
    def _():
        acc_ref[...] += part

    @pl.when(j == pl.num_programs(1) - 1)
    def _():
        y = x_ref[...] + 0.5 * acc_ref[...]
        if final_norm:
            ms = jnp.mean(y * y, axis=-1, keepdims=True)
            y = y * lax.rsqrt(ms + EPS) * fg_ref[...]
        o_ref[...] = y


def _ffn(x, g, w_in, w_out, final_g=None, *, tm, fc):
    s, d = x.shape
    f = w_out.shape[0]
    nf = f // fc
    fg = jnp.ones((1, d), F32) if final_g is None else final_g.reshape(1, d)
    return pl.pallas_call(
        functools.partial(_ffn_body, final_norm=final_g is not None),
        grid=(s // tm, nf),
        in_specs=[
            pl.BlockSpec((tm, d), lambda i, j: (i, 0)),
            pl.BlockSpec((1, d), lambda i, j: (0, 0)),
            pl.BlockSpec((d, fc), lambda i, j: (0, j)),
            pl.BlockSpec((d, fc), lambda i, j: (0, nf + j)),
            pl.BlockSpec((fc, d), lambda i, j: (j, 0)),
            pl.BlockSpec((1, d), lambda i, j: (0, 0)),
        ],
        out_specs=pl.BlockSpec((tm, d), lambda i, j: (i, 0)),
        out_shape=jax.ShapeDtypeStruct((s, d), F32),
        scratch_shapes=[pltpu.VMEM((tm, d), BF16), pltpu.VMEM((tm, d), F32)],
        compiler_params=_params("parallel", "arbitrary"),
        name="ffn",
    )(x, g.reshape(1, d), w_in, w_in, w_out, fg)


def _proj_body(*refs, segs, has_rope, chunk):
    n_in = 6 if has_rope else 3
    x_ref, g_ref, w_ref = refs[:3]
    outs = refs[n_in:]
    hn = _rms_bf16(x_ref[...], g_ref[...])
    if has_rope:
        cos_t, sa_t, sb_t = (r[...] for r in refs[3:6])
    col = 0
    for o_ref, (width, rope) in zip(outs, segs):
        for c0 in range(0, width, chunk):
            cw = min(chunk, width - c0)
            y = _dot(hn, w_ref[:, col + c0:col + c0 + cw])
            if rope:
                rep = cw // LANE
                cos_c, sa_c, sb_c = (jnp.concatenate([t] * rep, axis=1) if rep > 1 else t
                                     for t in (cos_t, sa_t, sb_t))
                half = ROT_DIM // 2
                y = y * cos_c + pltpu.roll(y, half, axis=1) * sa_c + pltpu.roll(y, cw - half, axis=1) * sb_c
            o_ref[:, c0:c0 + cw] = y.astype(o_ref.dtype)
        col += width


def _proj(x, g, w, segs, rope_tables=None, *, tm, chunk=256):
    s, d = x.shape
    n = w.shape[1]
    assert n == sum(sg[0] for sg in segs)
    has_rope = rope_tables is not None
    in_specs = [
        pl.BlockSpec((tm, d), lambda i: (i, 0)),
        pl.BlockSpec((1, d), lambda i: (0, 0)),
        _const_spec((d, n)),
    ]
    args = [x, g.reshape(1, d), w]
    if has_rope:
        in_specs += [pl.BlockSpec((tm, LANE), lambda i: (i, 0))] * 3
        args += list(rope_tables)
    return pl.pallas_call(
        functools.partial(_proj_body, segs=tuple((sg[0], sg[2]) for sg in segs), has_rope=has_rope,
                          chunk=chunk),
        grid=(s // tm,),
        in_specs=in_specs,
        out_specs=[pl.BlockSpec((tm, sg[0]), lambda i: (i, 0)) for sg in segs],
        out_shape=[jax.ShapeDtypeStruct((s, sg[0]), sg[1]) for sg in segs],
        compiler_params=_params("parallel"),
        name="proj",
    )(*args)


def _fgate_body(f_ref, b_ref, o_ref, carry_ref):
    i = pl.program_id(0)

    @pl.when(i == 0)
    def _():
        carry_ref[...] = jnp.zeros_like(carry_ref)

    z = f_ref[...] + b_ref[...]
    x = jnp.minimum(z, 0.0) - jnp.log(1.0 + jnp.exp(-jnp.abs(z)))
    tm = x.shape[0]
    row = lax.broadcasted_iota(jnp.int32, x.shape, 0)
    sh = 1
    while sh < tm:
        x = x + jnp.where(row >= sh, pltpu.roll(x, sh, axis=0), 0.0)
        sh *= 2
    x = x + carry_ref[...]
    carry_ref[...] = x[tm - 1:tm, :]
    o_ref[0] = x.T[:2 * SUBLANE, :]


def _fgate(f, b, *, tk):
    s = f.shape[0]
    return pl.pallas_call(
        _fgate_body,
        grid=(s // tk,),
        in_specs=[pl.BlockSpec((tk, LANE), lambda i: (i, 0)), pl.BlockSpec((1, LANE), lambda i: (0, 0))],
        out_specs=pl.BlockSpec((1, 2 * SUBLANE, tk), lambda i: (i, 0, 0)),
        out_shape=jax.ShapeDtypeStruct((s // tk, 2 * SUBLANE, tk), F32),
        scratch_shapes=[pltpu.VMEM((1, LANE), F32)],
        compiler_params=_params("arbitrary"),
        name="fgate",
    )(f, b)


UNDERFLOW = 110.0


def _fox_body(q_ref, k_ref, v_ref, ft_ref, fend_ref, o_ref, m_ref, l_ref, acc_ref, kmax2_ref, *, tq, nhs):
    hp = pl.program_id(0)
    qi = pl.program_id(1)
    m_ref[...] = jnp.full_like(m_ref, NEG)
    l_ref[...] = jnp.zeros_like(l_ref)
    acc_ref[...] = jnp.zeros_like(acc_ref)
    cols = [slice(b * LANE, (b + 1) * LANE) for b in range(nhs)]

    @pl.when(qi == 0)
    def _():
        for b in range(nhs):
            def kmax_body(c, mx, b=b):
                kf = k_ref[pl.ds(pl.multiple_of(c * tq, tq), tq), cols[b]].astype(F32)
                return jnp.maximum(mx, jnp.sum(kf * kf, axis=1, keepdims=True))

            mx = lax.fori_loop(0, k_ref.shape[0] // tq, kmax_body, jnp.zeros((tq, 1), F32))
            kmax2_ref[b] = jnp.max(mx, axis=0, keepdims=True)

    qs, f0s, kb_lo = [], [], qi
    for b in range(nhs):
        h = hp * nhs + b
        q = q_ref[:, cols[b]] * SCALE
        f0 = ft_ref[qi, pl.ds(h, 1), :][:, 0:1]
        qf = q.astype(F32)
        qmax2 = jnp.max(jnp.sum(qf * qf, axis=1, keepdims=True), axis=0, keepdims=True)
        bound = f0 + (UNDERFLOW + 2.0 * jnp.sqrt(qmax2 * kmax2_ref[b]))
        fend = fend_ref[pl.ds(h, 1), :]
        blk = lax.broadcasted_iota(jnp.int32, fend.shape, 1)
        kb_lo = jnp.minimum(kb_lo, jnp.sum(jnp.where((fend > bound) & (blk < qi), 1, 0)))
        qs.append(q)
        f0s.append(f0)

    def step(kb, causal):
        start = pl.multiple_of(kb * tq, tq)
        for b in range(nhs):
            h = hp * nhs + b
            k = k_ref[pl.ds(start, tq), cols[b]]
            v = v_ref[pl.ds(start, tq), cols[b]]
            s = _dot_nt(qs[b], k) + (f0s[b] - ft_ref[kb, pl.ds(h, 1), :])
            if causal:
                r = lax.broadcasted_iota(jnp.int32, s.shape, 0)
                c = lax.broadcasted_iota(jnp.int32, s.shape, 1)
                s = jnp.where(c <= r, s, NEG)
            m_old = m_ref[b]
            m_new = jnp.maximum(m_old, jnp.max(s, axis=1, keepdims=True))
            alpha = jnp.exp(m_old - m_new)
            p = jnp.exp(s - m_new)
            l_ref[b] = alpha * l_ref[b] + jnp.sum(p, axis=1, keepdims=True)
            acc_ref[b] = alpha * acc_ref[b] + _dot(p.astype(BF16), v)
            m_ref[b] = m_new

    def loop_body(kb, carry):
        step(kb, False)
        return carry

    lax.fori_loop(kb_lo, qi, loop_body, 0)
    step(qi, True)
    for b in range(nhs):
        o_ref[:, cols[b]] = (acc_ref[b] / l_ref[b]).astype(o_ref.dtype)


def _fox(q, k, v, ft, *, tq, nhs=2):
    s = q.shape[0]
    nh = q.shape[1] // LANE
    nq = s // tq
    assert nq <= LANE and nh % nhs == 0
    w = nhs * LANE
    fend = jnp.pad(ft[:, :, tq - 1].T, ((0, 0), (0, LANE - nq)))
    return pl.pallas_call(
        functools.partial(_fox_body, tq=tq, nhs=nhs),
        grid=(nh // nhs, nq),
        in_specs=[
            pl.BlockSpec((tq, w), lambda h, i: (i, h)),
            pl.BlockSpec((s, w), lambda h, i: (0, h)),
            pl.BlockSpec((s, w), lambda h, i: (0, h)),
            _const_spec(ft.shape),
            _const_spec(fend.shape),
        ],
        out_specs=pl.BlockSpec((tq, w), lambda h, i: (i, h)),
        out_shape=jax.ShapeDtypeStruct((s, nh * LANE), BF16),
        scratch_shapes=[pltpu.VMEM((nhs, tq, 1), F32), pltpu.VMEM((nhs, tq, 1), F32),
                        pltpu.VMEM((nhs, tq, LANE), F32), pltpu.VMEM((nhs, 1, 1), F32)],
        compiler_params=_params("arbitrary", "arbitrary"),
        name="fox",
    )(q, k, v, ft, fend)


def _conv_body(bg_ref, cg_ref, v_ref, cgh_ref, vh_ref, w_ref, o_ref):
    i = pl.program_id(0)
    u = cg_ref[...] * v_ref[...]
    halo = jnp.where(i > 0, cgh_ref[...] * vh_ref[...], 0.0)
    ext = jnp.concatenate([halo, u], axis=0)
    u1 = pltpu.roll(ext, 1, axis=0)[SUBLANE:, :]
    u2 = pltpu.roll(ext, 2, axis=0)[SUBLANE:, :]
    w = w_ref[...]
    y = w[0:1, :] * u2 + w[1:2, :] * u1 + w[2:3, :] * u
    o_ref[...] = (bg_ref[...] * y).astype(o_ref.dtype)


def _conv(bg, cg, v, w, *, tm):
    s, c = bg.shape
    w8 = jnp.zeros((SUBLANE, c), F32).at[:CONV_WIDTH].set(w)
    per = tm // SUBLANE
    main = pl.BlockSpec((tm, c), lambda i: (i, 0))
    halo = pl.BlockSpec((SUBLANE, c), lambda i: (jnp.maximum(i * per - 1, 0), 0))
    return pl.pallas_call(
        _conv_body,
        grid=(s // tm,),
        in_specs=[main, main, main, halo, halo, pl.BlockSpec((SUBLANE, c), lambda i: (0, 0))],
        out_specs=main,
        out_shape=jax.ShapeDtypeStruct((s, c), BF16),
        compiler_params=_params("parallel"),
        name="conv",
    )(bg, cg, v, cg, v, w8)


def _cmp_body(t_ref, w1_ref, pe_ref, w2_ref, on_ref, ot_ref):
    t = t_ref[0, 0]
    n16 = t.shape[0]
    half = w1_ref.shape[1] // 2
    a = _dot(t, w1_ref[0, :half, :])
    b = _dot(t, w1_ref[0, half:, :])
    pec = _dot(pe_ref[0], w1_ref[0])[0:1, :]
    pre = a + pltpu.roll(b, n16 - 1, axis=0) + pec
    hid = (pre * jax.nn.sigmoid(pre)).astype(BF16)
    out = _dot(hid, w2_ref[0])
    on_ref[0, 0] = out.astype(on_ref.dtype)
    ot_ref[0, 0] = out.T.astype(ot_ref.dtype)


def _cmp(t16, w1, pe8, w2d):
    _, g, n16, kk = t16.shape
    return pl.pallas_call(
        _cmp_body,
        grid=(2, g),
        in_specs=[
            pl.BlockSpec((1, 1, n16, kk), lambda a, b: (a, b, 0, 0)),
            pl.BlockSpec((1, 2 * kk, HEAD_DIM), lambda a, b: (a, 0, 0)),
            pl.BlockSpec((1, SUBLANE, 2 * kk), lambda a, b: (a, 0, 0)),
            pl.BlockSpec((1, HEAD_DIM, LANE), lambda a, b: (a, 0, 0)),
        ],
        out_specs=[
            pl.BlockSpec((1, 1, n16, LANE), lambda a, b: (a, b, 0, 0)),
            pl.BlockSpec((1, 1, LANE, n16), lambda a, b: (a, b, 0, 0)),
        ],
        out_shape=[
            jax.ShapeDtypeStruct((2, g, n16, LANE), BF16),
            jax.ShapeDtypeStruct((2, g, LANE, n16), BF16),
        ],
        compiler_params=_params("parallel", "parallel"),
        name="nsa_cmp",
    )(t16, w1, pe8, w2d)


def _stack_heads(q_blk):
    return jnp.concatenate([q_blk[:, j * LANE:(j + 1) * LANE] for j in range(NSA_GROUP)], axis=0) * SCALE


def _unstack_heads(o3):
    tq = o3.shape[0] // NSA_GROUP
    return jnp.concatenate([o3[j * tq:(j + 1) * tq, :] for j in range(NSA_GROUP)], axis=1)


def _cmpattn_body(q_ref, kc_ref, vct_ref, ov_ref, oc_ref, sel_ref, *, tq, n_top):
    qi = pl.program_id(1)
    t0 = qi * tq
    q3 = _stack_heads(q_ref[...])
    kc = kc_ref[0, 0]
    n16 = kc.shape[0]
    st = _dot_nt(kc, q3)
    n_idx = lax.broadcasted_iota(jnp.int32, st.shape, 0)
    lane = lax.broadcasted_iota(jnp.int32, st.shape, 1)
    tpos = t0 + (lane & (tq - 1))
    mask = n_idx * CMP_STRIDE + (CMP_LEN - 1) <= tpos
    st = jnp.where(mask, st, NEG)
    m = jnp.max(st, axis=0, keepdims=True)
    p = jnp.where(mask, jnp.exp(st - m), 0.0)
    l = jnp.sum(p, axis=0, keepdims=True)
    inv_l = 1.0 / jnp.maximum(l, jnp.finfo(F32).tiny)
    pn = p * inv_l
    oct_ = _dot(vct_ref[0, 0], p.astype(BF16)) * inv_l
    oc_ref[...] = _unstack_heads(oct_.T).astype(oc_ref.dtype)

    psum = pn[:, 0:tq]
    for j in range(1, NSA_GROUP):
        psum = psum + pn[:, j * tq:(j + 1) * tq]
    p_hi = psum.astype(BF16)
    p_lo = (psum - p_hi.astype(F32)).astype(BF16)
    ov = ov_ref[...]
    imp = _dot(ov, p_hi) + _dot(ov, p_lo)

    jf = lax.broadcasted_iota(jnp.int32, imp.shape, 0).astype(F32)
    tl = t0 + lax.broadcasted_iota(jnp.int32, imp.shape, 1)
    tblk = jnp.right_shift(tl, SLC_LEN.bit_length() - 1).astype(F32)
    valid = jf <= tblk
    forced = (jf == 0.0) | (jf == tblk) | (jf == tblk - 1.0)
    score = jnp.where(valid, jnp.where(forced, jnp.inf, imp), -jnp.inf)
    sel = jnp.zeros(imp.shape, F32)
    big = float(imp.shape[0])
    for _ in range(n_top):
        mx = jnp.max(score, axis=0, keepdims=True)
        idx = jnp.min(jnp.where(score == mx, jf, big), axis=0, keepdims=True)
        hit = jf == idx
        sel = jnp.where(hit & (mx > -jnp.inf), 1.0, sel)
        score = jnp.where(hit, -jnp.inf, score)
    sel_ref[0, 0] = sel


def _cmpattn(q, kcmp, vcmpt, ov, *, tq, n_top):
    s = q.shape[0]
    g, n16 = kcmp.shape[1], kcmp.shape[2]
    n_slc = ov.shape[0]
    nq = s // tq
    gw = NSA_GROUP * LANE
    return pl.pallas_call(
        functools.partial(_cmpattn_body, tq=tq, n_top=n_top),
        grid=(g, nq),
        in_specs=[
            pl.BlockSpec((tq, gw), lambda a, i: (i, a)),
            pl.BlockSpec((1, 1, n16, LANE), lambda a, i: (0, a, 0, 0)),
            pl.BlockSpec((1, 1, LANE, n16), lambda a, i: (1, a, 0, 0)),
            _const_spec(ov.shape),
        ],
        out_specs=[
            pl.BlockSpec((tq, gw), lambda a, i: (i, a)),
            pl.BlockSpec((1, 1, n_slc, tq), lambda a, i: (a, i, 0, 0)),
        ],
        out_shape=[
            jax.ShapeDtypeStruct((s, g * gw), F32),
            jax.ShapeDtypeStruct((g, nq, n_slc, tq), F32),
        ],
        compiler_params=_params("parallel", "arbitrary"),
        name="nsa_cmpattn",
    )(q, kcmp, vcmpt, ov)


def _selattn_body(q_ref, k_ref, vt_ref, sel_ref, o_ref, m_ref, l_ref, acc_ref, *, tq, kc, ng):
    qi = pl.program_id(1)
    gw = NSA_GROUP * LANE
    q3s = [_stack_heads(q_ref[:, b * gw:(b + 1) * gw]) for b in range(ng)]
    m_ref[...] = jnp.full_like(m_ref, NEG)
    l_ref[...] = jnp.zeros_like(l_ref)
    acc_ref[...] = jnp.zeros_like(acc_ref)
    bpc = kc // SLC_LEN
    n_full = (qi * tq) // kc

    def chunk(c, causal):
        start = pl.multiple_of(c * kc, kc)
        for b in range(ng):
            k = k_ref[pl.ds(start, kc), b * LANE:(b + 1) * LANE]
            st = _dot_nt(k, q3s[b])
            srows = sel_ref[b, 0, pl.ds(pl.multiple_of(c * bpc, bpc), bpc), :]
            pen = (jnp.concatenate([srows] * NSA_GROUP, axis=1) - 1.0) * (-NEG)
            st = st + jnp.concatenate(
                [jnp.broadcast_to(pen[r:r + 1, :], (SLC_LEN, gw)) for r in range(bpc)], axis=0)
            if causal:
                kpos = c * kc + lax.broadcasted_iota(jnp.int32, st.shape, 0)
                tpos = qi * tq + (lax.broadcasted_iota(jnp.int32, st.shape, 1) & (tq - 1))
                st = jnp.where(kpos <= tpos, st, NEG)
            m_old = m_ref[b]
            m_new = jnp.maximum(m_old, jnp.max(st, axis=0, keepdims=True))
            alpha = jnp.exp(m_old - m_new)
            p = jnp.exp(st - m_new)
            l_ref[b] = alpha * l_ref[b] + jnp.sum(p, axis=0, keepdims=True)
            acc_ref[b] = alpha * acc_ref[b] + _dot(vt_ref[b, c], p.astype(BF16))
            m_ref[b] = m_new

    def loop_body(c, carry):
        chunk(c, False)
        return carry

    lax.fori_loop(0, n_full, loop_body, 0)
    chunk(n_full, True)
    for b in range(ng):
        o_ref[:, b * gw:(b + 1) * gw] = _unstack_heads((acc_ref[b] / l_ref[b]).T).astype(o_ref.dtype)


def _selattn(q, ks, vst, sel, *, tq, ng):
    s = q.shape[0]
    g, nq, n_slc, _ = sel.shape
    _, nchunk, _, kc = vst.shape
    gw = NSA_GROUP * LANE
    return pl.pallas_call(
        functools.partial(_selattn_body, tq=tq, kc=kc, ng=ng),
        grid=(g // ng, nq),
        in_specs=[
            pl.BlockSpec((tq, ng * gw), lambda a, i: (i, a)),
            pl.BlockSpec((s, ng * LANE), lambda a, i: (0, a)),
            pl.BlockSpec((ng, nchunk, LANE, kc), lambda a, i: (a, 0, 0, 0)),
            pl.BlockSpec((ng, 1, n_slc, tq), lambda a, i: (a, i, 0, 0)),
        ],
        out_specs=pl.BlockSpec((tq, ng * gw), lambda a, i: (i, a)),
        out_shape=jax.ShapeDtypeStruct((s, g * gw), F32),
        scratch_shapes=[pltpu.VMEM((ng, 1, gw), F32), pltpu.VMEM((ng, 1, gw), F32),
                        pltpu.VMEM((ng, LANE, gw), F32)],
        compiler_params=_params("parallel", "arbitrary"),
        name="nsa_selattn",
    )(q, ks, vst, sel)


def _winattn_body(*refs, tq, nwb):
    q_ref = refs[0]
    k_refs = refs[1:2 + nwb]
    v_refs = refs[2 + nwb:3 + 2 * nwb]
    o_ref = refs[3 + 2 * nwb]
    qi = pl.program_id(1)
    q3 = _stack_heads(q_ref[...])
    k = jnp.concatenate([r[...] for r in k_refs], axis=0)
    v = jnp.concatenate([r[...] for r in v_refs], axis=0)
    s = _dot_nt(q3, k)
    tpos = qi * tq + (lax.broadcasted_iota(jnp.int32, s.shape, 0) & (tq - 1))
    kpos = (qi - nwb) * tq + lax.broadcasted_iota(jnp.int32, s.shape, 1)
    mask = (kpos >= 0) & (kpos <= tpos) & (tpos - kpos < WINDOW)
    s = jnp.where(mask, s, NEG)
    m = jnp.max(s, axis=1, keepdims=True)
    p = jnp.where(mask, jnp.exp(s - m), 0.0)
    l = jnp.sum(p, axis=1, keepdims=True)
    o3 = _dot(p.astype(BF16), v) / l
    o_ref[...] = _unstack_heads(o3).astype(o_ref.dtype)


def _winattn(q, kw, vw, *, tq):
    s = q.shape[0]
    g = kw.shape[1] // LANE
    nwb = WINDOW // tq
    gw = NSA_GROUP * LANE

    def kv_spec(back):
        return pl.BlockSpec((tq, LANE), lambda a, i: (jnp.maximum(i - back, 0), a))

    kv_specs = [kv_spec(nwb - b) for b in range(nwb + 1)]
    return pl.pallas_call(
        functools.partial(_winattn_body, tq=tq, nwb=nwb),
        grid=(g, s // tq),
        in_specs=[pl.BlockSpec((tq, gw), lambda a, i: (i, a))] + kv_specs + kv_specs,
        out_specs=pl.BlockSpec((tq, gw), lambda a, i: (i, a)),
        out_shape=jax.ShapeDtypeStruct((s, g * gw), F32),
        compiler_params=_params("parallel", "parallel"),
        name="nsa_winattn",
    )(q, *([kw] * (nwb + 1)), *([vw] * (nwb + 1)))


def _outproj_body(*refs, mode):
    if mode == "nsa":
        x_ref, oc_ref, os_ref, ow_ref, gl_ref, qx_ref, mk_ref, mv_ref, wmix_ref, wmem_ref, o_ref = refs
        gate = jax.nn.sigmoid(gl_ref[...])
        parts = []
        for hd in range(MIX_HEADS):
            sl = slice(hd * LANE, (hd + 1) * LANE)
            acc = None
            for b, br in enumerate((oc_ref, os_ref, ow_ref)):
                c = b * MIX_HEADS + hd
                term = gate[:, c:c + 1] * br[:, sl]
                acc = term if acc is None else acc + term
            parts.append(acc.astype(BF16))
        ymix = jnp.concatenate(parts, axis=1)
    else:
        x_ref, ymix_ref, qx_ref, mk_ref, mv_ref, wmix_ref, wmem_ref, o_ref = refs
        ymix = ymix_ref[...]
    y = _dot(ymix, wmix_ref[...])
    for hd in range(MEM_HEADS):
        sl = slice(hd * LANE, (hd + 1) * LANE)
        s = _dot_nt(qx_ref[:, sl], mk_ref[:, sl]) * SCALE
        m = jnp.max(s, axis=1, keepdims=True)
        p = jnp.exp(s - m)
        p = p / jnp.sum(p, axis=1, keepdims=True)
        ymem = _dot(p.astype(BF16), mv_ref[:, sl]).astype(BF16)
        y = y + _dot(ymem, wmem_ref[sl, :])
    o_ref[...] = x_ref[...] + y


def _outproj(x, mix_args, qx, mk, mv, wmix, wmem, *, mode, tm):
    s, d = x.shape
    row = lambda w: pl.BlockSpec((tm, w), lambda i: (i, 0))
    mix_specs = [row(a.shape[1]) for a in mix_args]
    return pl.pallas_call(
        functools.partial(_outproj_body, mode=mode),
        grid=(s // tm,),
        in_specs=[row(d)] + mix_specs + [row(qx.shape[1]), _const_spec(mk.shape), _const_spec(mv.shape),
                                         _const_spec(wmix.shape), _const_spec(wmem.shape)],
        out_specs=row(d),
        out_shape=jax.ShapeDtypeStruct((s, d), F32),
        compiler_params=_params("parallel"),
        name="outproj",
    )(x, *mix_args, qx, mk, mv, wmix, wmem)


def _pad_heads_cols(w, n_heads):
    d = w.shape[0]
    w = w.reshape(d, n_heads, HEAD_DIM)
    return jnp.pad(w, ((0, 0), (0, 0), (0, LANE - HEAD_DIM))).reshape(d, n_heads * LANE)


def _pad_heads_rows(w, n_heads):
    d = w.shape[1]
    w = w.reshape(n_heads, HEAD_DIM, d)
    return jnp.pad(w, ((0, 0), (0, LANE - HEAD_DIM), (0, 0))).reshape(n_heads * LANE, d)


def _pad_cols(w, width):
    return jnp.pad(w, ((0, 0), (0, width - w.shape[1])))


def _rope_tables(s):
    half = ROT_DIM // 2
    inv = ROPE_THETA ** (-jnp.arange(half, dtype=F32) / half)
    ang = jnp.arange(s, dtype=F32)[:, None] * inv[None, :]
    cos, sin = jnp.cos(ang), jnp.sin(ang)
    z = lambda n: jnp.zeros((s, n), F32)
    cos_t = jnp.concatenate([cos, cos, jnp.ones((s, LANE - ROT_DIM), F32)], axis=1)
    sa_t = jnp.concatenate([z(half), sin, z(LANE - ROT_DIM)], axis=1)
    sb_t = jnp.concatenate([-sin, z(LANE - half)], axis=1)
    return cos_t, sa_t, sb_t


def _pick(n, candidates):
    for c in candidates:
        if n % c == 0:
            return c
    raise ValueError(f"no tile size for {n}")


def kernel(x, mem, ffn1_norm, ffn1_w_in, ffn1_w_out, mix_norm, mix_w_out, mem_norm, mem_w_kv, fox_w_in,
           fox_b_f, conv_w_in, conv_w, nsa_w_in, nsa_cmp_pos, nsa_cmp_w1, nsa_cmp_w2, ffn2_norm, ffn2_w_in,
           ffn2_w_out, final_norm):
    b, s, d = x.shape
    assert b == 1
    depth = ffn1_norm.shape[0]
    mix_w = MIX_HEADS * HEAD_DIM
    mem_w = MEM_HEADS * HEAD_DIM
    kv_w = NSA_KV_HEADS * HEAD_DIM
    d_ff = ffn1_w_out.shape[1]
    tm_ffn = _pick(s, (1024, 512, 256, 128))
    fc = next(c for c in (d_ff // 4, d_ff // 2, 256, LANE) if d_ff % c == 0 and c % LANE == 0)
    tm = _pick(s, (512, 256, 128))
    tq_fox = _pick(s, (512, 256, 128))
    tq_nsa = 128
    kc_sel = _pick(s, (512, 256, 128))
    tq_win = _pick(s, (256, 128))
    m_len = mem.shape[1]

    xs = x[0]
    mem_s = mem[0]
    rope_t = None
    for i in range(depth):
        xs = _ffn(xs, ffn1_norm[i], ffn1_w_in[i].astype(BF16), ffn1_w_out[i].astype(BF16), tm=tm_ffn, fc=fc)

        wkv = mem_w_kv[i]
        wkv_p = jnp.concatenate([_pad_heads_cols(wkv[:, :mem_w], MEM_HEADS),
                                 _pad_heads_cols(wkv[:, mem_w:], MEM_HEADS)], axis=1).astype(BF16)
        mk, mv = _proj(mem_s, mem_norm, wkv_p, [(MEM_HEADS * LANE, BF16, False)] * 2, tm=m_len)

        w_out = mix_w_out[i]
        wmem = _pad_heads_rows(w_out[mix_w:], MEM_HEADS).astype(BF16)
        kind, j = i % N_MIXERS, i // N_MIXERS
        if kind == 0:
            w = fox_w_in[j]
            wq, wk, wv = (w[:, a * mix_w:(a + 1) * mix_w] for a in range(3))
            wf = w[:, 3 * mix_w:3 * mix_w + MIX_HEADS]
            wqx = w[:, 3 * mix_w + MIX_HEADS:]
            wp = jnp.concatenate([_pad_heads_cols(wq, MIX_HEADS), _pad_heads_cols(wk, MIX_HEADS),
                                  _pad_heads_cols(wv, MIX_HEADS), _pad_heads_cols(wqx, MEM_HEADS),
                                  _pad_cols(wf, LANE)], axis=1).astype(BF16)
            hw = MIX_HEADS * LANE
            q, k, v, qx, f = _proj(xs, mix_norm[i], wp,
                                   [(hw, BF16, False)] * 3 + [(MEM_HEADS * LANE, BF16, False), (LANE, F32, False)],
                                   tm=tm)
            ft = _fgate(f, _pad_cols(fox_b_f[j].reshape(1, MIX_HEADS), LANE), tk=tq_fox)
            ymix = _fox(q, k, v, ft, tq=tq_fox)
            wmix = _pad_heads_rows(w_out[:mix_w], MIX_HEADS).astype(BF16)
            xs = _outproj(xs, [ymix], qx, mk, mv, wmix, wmem, mode="plain", tm=tm)
        elif kind == 1:
            w = conv_w_in[j]
            wp = jnp.concatenate([w[:, :3 * mix_w], _pad_heads_cols(w[:, 3 * mix_w:], MEM_HEADS)],
                                 axis=1).astype(BF16)
            bg, cg, v, qx = _proj(xs, mix_norm[i], wp,
                                  [(mix_w, F32, False)] * 3 + [(MEM_HEADS * LANE, BF16, False)], tm=tm)
            ymix = _conv(bg, cg, v, conv_w[j], tm=tm)
            xs = _outproj(xs, [ymix], qx, mk, mv, w_out[:mix_w].astype(BF16), wmem, mode="plain", tm=tm)
        else:
            w = nsa_w_in[j]
            off = mix_w
            wq = w[:, :mix_w]
            kvs = []
            for _ in range(6):
                kvs.append(_pad_heads_cols(w[:, off:off + kv_w], NSA_KV_HEADS))
                off += kv_w
            wgl = w[:, off:off + 3 * MIX_HEADS]
            wqx = w[:, off + 3 * MIX_HEADS:]
            wp = jnp.concatenate([_pad_heads_cols(wq, MIX_HEADS)] + kvs +
                                 [_pad_heads_cols(wqx, MEM_HEADS), _pad_cols(wgl, LANE)], axis=1).astype(BF16)
            if rope_t is None:
                rope_t = _rope_tables(s)
            gwid = NSA_KV_HEADS * LANE
            segs = [(MIX_HEADS * LANE, BF16, True)]
            segs += [(gwid, BF16, a % 2 == 0) for a in range(6)]
            segs += [(MEM_HEADS * LANE, BF16, False), (LANE, F32, False)]
            q, kc, vc, ks_, vs_, kw, vw, qx, gl = _proj(xs, mix_norm[i], wp, segs, rope_t, tm=tm)

            n16 = s // CMP_STRIDE
            sub = CMP_STRIDE

            def to_t16(t):
                t = t.reshape(n16, sub, NSA_KV_HEADS, LANE)[..., :HEAD_DIM]
                return t.transpose(2, 0, 1, 3).reshape(NSA_KV_HEADS, n16, sub * HEAD_DIM)

            t16 = jnp.stack([to_t16(kc), to_t16(vc)])
            w1 = nsa_cmp_w1[j].astype(BF16)
            pe8 = jnp.zeros((2, SUBLANE, CMP_LEN * HEAD_DIM), F32).at[:, 0].set(
                nsa_cmp_pos[j].reshape(2, CMP_LEN * HEAD_DIM)).astype(BF16)
            w2d = jnp.pad(nsa_cmp_w2[j], ((0, 0), (0, 0), (0, LANE - HEAD_DIM))).astype(BF16)
            cmp_n, cmp_t = _cmp(t16, w1, pe8, w2d)

            n_slc = s // SLC_LEN
            n_top = min(SLC_TOPN, n_slc)
            jj = jnp.arange(n_slc)[:, None] * SLC_LEN
            nn = jnp.arange(n16)[None, :] * CMP_STRIDE
            ov = ((nn < jj + SLC_LEN) & (nn + CMP_LEN > jj) & (nn + CMP_LEN <= s)).astype(BF16)
            oc, sel = _cmpattn(q, cmp_n, cmp_t, ov, tq=tq_nsa, n_top=n_top)

            vst = vs_.reshape(s // kc_sel, kc_sel, NSA_KV_HEADS, LANE).transpose(2, 0, 3, 1)
            osel = _selattn(q, ks_, vst, sel, tq=tq_nsa, ng=2)
            ow = _winattn(q, kw, vw, tq=tq_win)
            wmix = _pad_heads_rows(w_out[:mix_w], MIX_HEADS).astype(BF16)
            xs = _outproj(xs, [oc, osel, ow, gl], qx, mk, mv, wmix, wmem, mode="nsa", tm=tm)

        last = i == depth - 1
        xs = _ffn(xs, ffn2_norm[i], ffn2_w_in[i].astype(BF16), ffn2_w_out[i].astype(BF16),
                  final_norm if last else None, tm=tm_ffn, fc=fc)
    return xs[None]
```

```python
import functools
import math

import jax
import jax.numpy as jnp
from jax import lax
from jax.experimental import pallas as pl
from jax.experimental.pallas import tpu as pltpu

F32 = jnp.float32
BF16 = jnp.bfloat16

HEAD_DIM = 64
MIX_HEADS = 12
MEM_HEADS = 4
N_MIXERS = 3
ROT_DIM = HEAD_DIM // 4
ROPE_THETA = 500000.0
CONV_WIDTH = 3
NSA_KV_HEADS = 4
NSA_GROUP = MIX_HEADS // NSA_KV_HEADS
CMP_LEN = 32
CMP_STRIDE = 16
SLC_LEN = 64
SLC_TOPN = 16
WINDOW = 512
EPS = 1e-6

LANE = 128
SUBLANE = 8
VMEM_LIMIT_BYTES = 56 * 2**20
NEG = -1e30
SCALE = 1.0 / math.sqrt(HEAD_DIM)


def _params(*sem):
    return pltpu.CompilerParams(dimension_semantics=sem, vmem_limit_bytes=VMEM_LIMIT_BYTES)


def _const_spec(shape):
    zeros = (0,) * len(shape)
    return pl.BlockSpec(shape, lambda *_: zeros, pipeline_mode=pl.Buffered(1))


def _rms_bf16(x, g):
    ms = jnp.mean(x * x, axis=-1, keepdims=True)
    return (x * lax.rsqrt(ms + EPS) * g).astype(BF16)


def _dot(a, b):
    return jnp.dot(a, b, preferred_element_type=F32)


def _dot_nt(a, b):
    return lax.dot_general(a, b, (((1,), (1,)), ((), ())), preferred_element_type=F32)


def _ffn_body(x_ref, g_ref, wg_ref, wu_ref, wo_ref, fg_ref, o_ref, hn_ref, acc_ref, *, final_norm):
    j = pl.program_id(1)

    @pl.when(j == 0)
    def _():
        hn_ref[...] = _rms_bf16(x_ref[...], g_ref[...])

    hn = hn_ref[...]
    gate = _dot(hn, wg_ref[...])
    up = _dot(hn, wu_ref[...])
    act = (gate * jax.nn.sigmoid(gate) * up).astype(BF16)
    part = _dot(act, wo_ref[...])

    @pl.when(j == 0)
    def _():
        acc_ref[...] = part

    @pl.when(j > 0)
    def _():
        acc_ref[...] += part

    @pl.when(j == pl.num_programs(1) - 1)
    def _():
        y = x_ref[...] + 0.5 * acc_ref[...]
        if final_norm:
            ms = jnp.mean(y * y, axis=-1, keepdims=True)
            y = y * lax.rsqrt(ms + EPS) * fg_ref[...]
        o_ref[...] = y


def _ffn(x, g, w_in, w_out, final_g=None, *, tm, fc):
    s, d = x.shape
    f = w_out.shape[0]
    nf = f // fc
    fg = jnp.ones((1, d), F32) if final_g is None else final_g.reshape(1, d)
    return pl.pallas_call(
        functools.partial(_ffn_body, final_norm=final_g is not None),
        grid=(s // tm, nf),
        in_specs=[
            pl.BlockSpec((tm, d), lambda i, j: (i, 0)),
            pl.BlockSpec((1, d), lambda i, j: (0, 0)),
            pl.BlockSpec((d, fc), lambda i, j: (0, j)),
            pl.BlockSpec((d, fc), lambda i, j: (0, nf + j)),
            pl.BlockSpec((fc, d), lambda i, j: (j, 0)),
            pl.BlockSpec((1, d), lambda i, j: (0, 0)),
        ],
        out_specs=pl.BlockSpec((tm, d), lambda i, j: (i, 0)),
        out_shape=jax.ShapeDtypeStruct((s, d), F32),
        scratch_shapes=[pltpu.VMEM((tm, d), BF16), pltpu.VMEM((tm, d), F32)],
        compiler_params=_params("parallel", "arbitrary"),
        name="ffn",
    )(x, g.reshape(1, d), w_in, w_in, w_out, fg)


def _proj_body(*refs, segs, has_rope, chunk):
    n_in = 6 if has_rope else 3
    x_ref, g_ref, w_ref = refs[:3]
    outs = refs[n_in:]
    hn = _rms_bf16(x_ref[...], g_ref[...])
    if has_rope:
        cos_t, sa_t, sb_t = (r[...] for r in refs[3:6])
    col = 0
    for o_ref, (width, rope) in zip(outs, segs):
        for c0 in range(0, width, chunk):
            cw = min(chunk, width - c0)
            y = _dot(hn, w_ref[:, col + c0:col + c0 + cw])
            if rope:
                rep = cw // LANE
                cos_c, sa_c, sb_c = (jnp.concatenate([t] * rep, axis=1) if rep > 1 else t
                                     for t in (cos_t, sa_t, sb_t))
                half = ROT_DIM // 2
                y = y * cos_c + pltpu.roll(y, half, axis=1) * sa_c + pltpu.roll(y, cw - half, axis=1) * sb_c
            o_ref[:, c0:c0 + cw] = y.astype(o_ref.dtype)
        col += width


def _proj(x, g, w, segs, rope_tables=None, *, tm, chunk=256):
    s, d = x.shape
    n = w.shape[1]
    assert n == sum(sg[0] for sg in segs)
    has_rope = rope_tables is not None
    in_specs = [
        pl.BlockSpec((tm, d), lambda i: (i, 0)),
        pl.BlockSpec((1, d), lambda i: (0, 0)),
        _const_spec((d, n)),
    ]
    args = [x, g.reshape(1, d), w]
    if has_rope:
        in_specs += [pl.BlockSpec((tm, LANE), lambda i: (i, 0))] * 3
        args += list(rope_tables)
    return pl.pallas_call(
        functools.partial(_proj_body, segs=tuple((sg[0], sg[2]) for sg in segs), has_rope=has_rope,
                          chunk=chunk),
        grid=(s // tm,),
        in_specs=in_specs,
        out_specs=[pl.BlockSpec((tm, sg[0]), lambda i: (i, 0)) for sg in segs],
        out_shape=[jax.ShapeDtypeStruct((s, sg[0]), sg[1]) for sg in segs],
        compiler_params=_params("parallel"),
        name="proj",
    )(*args)


def _fgate_body(f_ref, b_ref, o_ref, carry_ref):
    i = pl.program_id(0)

    @pl.when(i == 0)
    def _():
        carry_ref[...] = jnp.zeros_like(carry_ref)

    z = f_ref[...] + b_ref[...]
    x = jnp.minimum(z, 0.0) - jnp.log(1.0 + jnp.exp(-jnp.abs(z)))
    tm = x.shape[0]
    row = lax.broadcasted_iota(jnp.int32, x.shape, 0)
    sh = 1
    while sh < tm:
        x = x + jnp.where(row >= sh, pltpu.roll(x, sh, axis=0), 0.0)
        sh *= 2
    x = x + carry_ref[...]
    carry_ref[...] = x[tm - 1:tm, :]
    o_ref[0] = x.T[:2 * SUBLANE, :]


def _fgate(f, b, *, tk):
    s = f.shape[0]
    return pl.pallas_call(
        _fgate_body,
        grid=(s // tk,),
        in_specs=[pl.BlockSpec((tk, LANE), lambda i: (i, 0)), pl.BlockSpec((1, LANE), lambda i: (0, 0))],
        out_specs=pl.BlockSpec((1, 2 * SUBLANE, tk), lambda i: (i, 0, 0)),
        out_shape=jax.ShapeDtypeStruct((s // tk, 2 * SUBLANE, tk), F32),
        scratch_shapes=[pltpu.VMEM((1, LANE), F32)],
        compiler_params=_params("arbitrary"),
        name="fgate",
    )(f, b)


UNDERFLOW = 110.0


def _fox_body(q_ref, k_ref, v_ref, ft_ref, fend_ref, o_ref, m_ref, l_ref, acc_ref, kmax2_ref, *, tq, nhs):
    hp = pl.program_id(0)
    qi = pl.program_id(1)
    m_ref[...] = jnp.full_like(m_ref, NEG)
    l_ref[...] = jnp.zeros_like(l_ref)
    acc_ref[...] = jnp.zeros_like(acc_ref)
    cols = [slice(b * LANE, (b + 1) * LANE) for b in range(nhs)]

    @pl.when(qi == 0)
    def _():
        for b in range(nhs):
            def kmax_body(c, mx, b=b):
                kf = k_ref[pl.ds(pl.multiple_of(c * tq, tq), tq), cols[b]].astype(F32)
                return jnp.maximum(mx, jnp.sum(kf * kf, axis=1, keepdims=True))

            mx = lax.fori_loop(0, k_ref.shape[0] // tq, kmax_body, jnp.zeros((tq, 1), F32))
            kmax2_ref[b] = jnp.max(mx, axis=0, keepdims=True)

    qs, f0s, kb_lo = [], [], qi
    for b in range(nhs):
        h = hp * nhs + b
        q = q_ref[:, cols[b]] * SCALE
        f0 = ft_ref[qi, pl.ds(h, 1), :][:, 0:1]
        qf = q.astype(F32)
        qmax2 = jnp.max(jnp.sum(qf * qf, axis=1, keepdims=True), axis=0, keepdims=True)
        bound = f0 + (UNDERFLOW + 2.0 * jnp.sqrt(qmax2 * kmax2_ref[b]))
        fend = fend_ref[pl.ds(h, 1), :]
        blk = lax.broadcasted_iota(jnp.int32, fend.shape, 1)
        kb_lo = jnp.minimum(kb_lo, jnp.sum(jnp.where((fend > bound) & (blk < qi), 1, 0)))
        qs.append(q)
        f0s.append(f0)

    def step(kb, causal):
        start = pl.multiple_of(kb * tq, tq)
        for b in range(nhs):
            h = hp * nhs + b
            k = k_ref[pl.ds(start, tq), cols[b]]
            v = v_ref[pl.ds(start, tq), cols[b]]
            s = _dot_nt(qs[b], k) + (f0s[b] - ft_ref[kb, pl.ds(h, 1), :])
            if causal:
                r = lax.broadcasted_iota(jnp.int32, s.shape, 0)
                c = lax.broadcasted_iota(jnp.int32, s.shape, 1)
                s = jnp.where(c <= r, s, NEG)
            m_old = m_ref[b]
            m_new = jnp.maximum(m_old, jnp.max(s, axis=1, keepdims=True))
            alpha = jnp.exp(m_old - m_new)
            p = jnp.exp(s - m_new)
            l_ref[b] = alpha * l_ref[b] + jnp.sum(p, axis=1, keepdims=True)
            acc_ref[b] = alpha * acc_ref[b] + _dot(p.astype(BF16), v)
            m_ref[b] = m_new

    def loop_body(kb, carry):
        step(kb, False)
        return carry

    lax.fori_loop(kb_lo, qi, loop_body, 0)
    step(qi, True)
    for b in range(nhs):
        o_ref[:, cols[b]] = (acc_ref[b] / l_ref[b]).astype(o_ref.dtype)


def _fox(q, k, v, ft, *, tq, nhs=1):
    s = q.shape[0]
    nh = q.shape[1] // LANE
    nq = s // tq
    assert nq <= LANE and nh % nhs == 0
    w = nhs * LANE
    fend = jnp.pad(ft[:, :, tq - 1].T, ((0, 0), (0, LANE - nq)))
    return pl.pallas_call(
        functools.partial(_fox_body, tq=tq, nhs=nhs),
        grid=(nh // nhs, nq),
        in_specs=[
            pl.BlockSpec((tq, w), lambda h, i: (i, h)),
            pl.BlockSpec((s, w), lambda h, i: (0, h)),
            pl.BlockSpec((s, w), lambda h, i: (0, h)),
            _const_spec(ft.shape),
            _const_spec(fend.shape),
        ],
        out_specs=pl.BlockSpec((tq, w), lambda h, i: (i, h)),
        out_shape=jax.ShapeDtypeStruct((s, nh * LANE), BF16),
        scratch_shapes=[pltpu.VMEM((nhs, tq, 1), F32), pltpu.VMEM((nhs, tq, 1), F32),
                        pltpu.VMEM((nhs, tq, LANE), F32), pltpu.VMEM((nhs, 1, 1), F32)],
        compiler_params=_params("arbitrary", "arbitrary"),
        name="fox",
    )(q, k, v, ft, fend)


def _conv_body(bg_ref, cg_ref, v_ref, cgh_ref, vh_ref, w_ref, o_ref):
    i = pl.program_id(0)
    u = cg_ref[...] * v_ref[...]
    halo = jnp.where(i > 0, cgh_ref[...] * vh_ref[...], 0.0)
    ext = jnp.concatenate([halo, u], axis=0)
    u1 = pltpu.roll(ext, 1, axis=0)[SUBLANE:, :]
    u2 = pltpu.roll(ext, 2, axis=0)[SUBLANE:, :]
    w = w_ref[...]
    y = w[0:1, :] * u2 + w[1:2, :] * u1 + w[2:3, :] * u
    o_ref[...] = (bg_ref[...] * y).astype(o_ref.dtype)


def _conv(bg, cg, v, w, *, tm):
    s, c = bg.shape
    w8 = jnp.zeros((SUBLANE, c), F32).at[:CONV_WIDTH].set(w)
    per = tm // SUBLANE
    main = pl.BlockSpec((tm, c), lambda i: (i, 0))
    halo = pl.BlockSpec((SUBLANE, c), lambda i: (jnp.maximum(i * per - 1, 0), 0))
    return pl.pallas_call(
        _conv_body,
        grid=(s // tm,),
        in_specs=[main, main, main, halo, halo, pl.BlockSpec((SUBLANE, c), lambda i: (0, 0))],
        out_specs=main,
        out_shape=jax.ShapeDtypeStruct((s, c), BF16),
        compiler_params=_params("parallel"),
        name="conv",
    )(bg, cg, v, cg, v, w8)


def _cmp_body(t_ref, w1_ref, pe_ref, w2_ref, on_ref, ot_ref):
    t = t_ref[0, 0]
    n16 = t.shape[0]
    half = w1_ref.shape[1] // 2
    a = _dot(t, w1_ref[0, :half, :])
    b = _dot(t, w1_ref[0, half:, :])
    pec = _dot(pe_ref[0], w1_ref[0])[0:1, :]
    pre = a + pltpu.roll(b, n16 - 1, axis=0) + pec
    hid = (pre * jax.nn.sigmoid(pre)).astype(BF16)
    out = _dot(hid, w2_ref[0])
    on_ref[0, 0] = out.astype(on_ref.dtype)
    ot_ref[0, 0] = out.T.astype(ot_ref.dtype)


def _cmp(t16, w1, pe8, w2d):
    _, g, n16, kk = t16.shape
    return pl.pallas_call(
        _cmp_body,
        grid=(2, g),
        in_specs=[
            pl.BlockSpec((1, 1, n16, kk), lambda a, b: (a, b, 0, 0)),
            pl.BlockSpec((1, 2 * kk, HEAD_DIM), lambda a, b: (a, 0, 0)),
            pl.BlockSpec((1, SUBLANE, 2 * kk), lambda a, b: (a, 0, 0)),
            pl.BlockSpec((1, HEAD_DIM, LANE), lambda a, b: (a, 0, 0)),
        ],
        out_specs=[
            pl.BlockSpec((1, 1, n16, LANE), lambda a, b: (a, b, 0, 0)),
            pl.BlockSpec((1, 1, LANE, n16), lambda a, b: (a, b, 0, 0)),
        ],
        out_shape=[
            jax.ShapeDtypeStruct((2, g, n16, LANE), BF16),
            jax.ShapeDtypeStruct((2, g, LANE, n16), BF16),
        ],
        compiler_params=_params("parallel", "parallel"),
        name="nsa_cmp",
    )(t16, w1, pe8, w2d)


def _stack_heads(q_blk):
    return jnp.concatenate([q_blk[:, j * LANE:(j + 1) * LANE] for j in range(NSA_GROUP)], axis=0) * SCALE


def _unstack_heads(o3):
    tq = o3.shape[0] // NSA_GROUP
    return jnp.concatenate([o3[j * tq:(j + 1) * tq, :] for j in range(NSA_GROUP)], axis=1)


def _cmpattn_body(q_ref, kc_ref, vct_ref, ov_ref, oc_ref, sel_ref, *, tq, n_top):
    qi = pl.program_id(1)
    t0 = qi * tq
    q3 = _stack_heads(q_ref[...])
    kc = kc_ref[0, 0]
    n16 = kc.shape[0]
    st = _dot_nt(kc, q3)
    n_idx = lax.broadcasted_iota(jnp.int32, st.shape, 0)
    lane = lax.broadcasted_iota(jnp.int32, st.shape, 1)
    tpos = t0 + (lane & (tq - 1))
    mask = n_idx * CMP_STRIDE + (CMP_LEN - 1) <= tpos
    st = jnp.where(mask, st, NEG)
    m = jnp.max(st, axis=0, keepdims=True)
    p = jnp.where(mask, jnp.exp(st - m), 0.0)
    l = jnp.sum(p, axis=0, keepdims=True)
    inv_l = 1.0 / jnp.maximum(l, jnp.finfo(F32).tiny)
    pn = p * inv_l
    oct_ = _dot(vct_ref[0, 0], p.astype(BF16)) * inv_l
    oc_ref[...] = _unstack_heads(oct_.T).astype(oc_ref.dtype)

    psum = pn[:, 0:tq]
    for j in range(1, NSA_GROUP):
        psum = psum + pn[:, j * tq:(j + 1) * tq]
    p_hi = psum.astype(BF16)
    p_lo = (psum - p_hi.astype(F32)).astype(BF16)
    ov = ov_ref[...]
    imp = _dot(ov, p_hi) + _dot(ov, p_lo)

    jf = lax.broadcasted_iota(jnp.int32, imp.shape, 0).astype(F32)
    tl = t0 + lax.broadcasted_iota(jnp.int32, imp.shape, 1)
    tblk = jnp.right_shift(tl, SLC_LEN.bit_length() - 1).astype(F32)
    valid = jf <= tblk
    forced = (jf == 0.0) | (jf == tblk) | (jf == tblk - 1.0)
    score = jnp.where(valid, jnp.where(forced, jnp.inf, imp), -jnp.inf)
    sel = jnp.zeros(imp.shape, F32)
    big = float(imp.shape[0])
    for _ in range(n_top):
        mx = jnp.max(score, axis=0, keepdims=True)
        idx = jnp.min(jnp.where(score == mx, jf, big), axis=0, keepdims=True)
        hit = jf == idx
        sel = jnp.where(hit & (mx > -jnp.inf), 1.0, sel)
        score = jnp.where(hit, -jnp.inf, score)
    sel_ref[0, 0] = sel


def _cmpattn(q, kcmp, vcmpt, ov, *, tq, n_top):
    s = q.shape[0]
    g, n16 = kcmp.shape[1], kcmp.shape[2]
    n_slc = ov.shape[0]
    nq = s // tq
    gw = NSA_GROUP * LANE
    return pl.pallas_call(
        functools.partial(_cmpattn_body, tq=tq, n_top=n_top),
        grid=(g, nq),
        in_specs=[
            pl.BlockSpec((tq, gw), lambda a, i: (i, a)),
            pl.BlockSpec((1, 1, n16, LANE), lambda a, i: (0, a, 0, 0)),
            pl.BlockSpec((1, 1, LANE, n16), lambda a, i: (1, a, 0, 0)),
            _const_spec(ov.shape),
        ],
        out_specs=[
            pl.BlockSpec((tq, gw), lambda a, i: (i, a)),
            pl.BlockSpec((1, 1, n_slc, tq), lambda a, i: (a, i, 0, 0)),
        ],
        out_shape=[
            jax.ShapeDtypeStruct((s, g * gw), F32),
            jax.ShapeDtypeStruct((g, nq, n_slc, tq), F32),
        ],
        compiler_params=_params("parallel", "arbitrary"),
        name="nsa_cmpattn",
    )(q, kcmp, vcmpt, ov)


def _selattn_body(q_ref, k_ref, vt_ref, sel_ref, o_ref, m_ref, l_ref, acc_ref, *, tq, kc, ng):
    qi = pl.program_id(1)
    gw = NSA_GROUP * LANE
    q3s = [_stack_heads(q_ref[:, b * gw:(b + 1) * gw]) for b in range(ng)]
    m_ref[...] = jnp.full_like(m_ref, NEG)
    l_ref[...] = jnp.zeros_like(l_ref)
    acc_ref[...] = jnp.zeros_like(acc_ref)
    bpc = kc // SLC_LEN
    n_full = (qi * tq) // kc

    def chunk(c, causal):
        start = pl.multiple_of(c * kc, kc)
        for b in range(ng):
            k = k_ref[pl.ds(start, kc), b * LANE:(b + 1) * LANE]
            st = _dot_nt(k, q3s[b])
            blk_rows = pl.ds(pl.multiple_of(c * bpc, bpc), bpc)
            srows = [sel_ref[b, u, blk_rows, :] for u in range(tq // sel_ref.shape[3])]
            pen = (jnp.concatenate(srows * NSA_GROUP, axis=1) - 1.0) * (-NEG)
            st = st + jnp.concatenate(
                [jnp.broadcast_to(pen[r:r + 1, :], (SLC_LEN, pen.shape[1])) for r in range(bpc)], axis=0)
            if causal:
                kpos = c * kc + lax.broadcasted_iota(jnp.int32, st.shape, 0)
                tpos = qi * tq + (lax.broadcasted_iota(jnp.int32, st.shape, 1) & (tq - 1))
                st = jnp.where(kpos <= tpos, st, NEG)
            m_old = m_ref[b]
            m_new = jnp.maximum(m_old, jnp.max(st, axis=0, keepdims=True))
            alpha = jnp.exp(m_old - m_new)
            p = jnp.exp(st - m_new)
            l_ref[b] = alpha * l_ref[b] + jnp.sum(p, axis=0, keepdims=True)
            acc_ref[b] = alpha * acc_ref[b] + _dot(vt_ref[b, c], p.astype(BF16))
            m_ref[b] = m_new

    def loop_body(c, carry):
        chunk(c, False)
        return carry

    lax.fori_loop(0, n_full, loop_body, 0)
    chunk(n_full, True)
    for b in range(ng):
        o_ref[:, b * gw:(b + 1) * gw] = _unstack_heads((acc_ref[b] / l_ref[b]).T).astype(o_ref.dtype)


def _selattn(q, ks, vst, sel, *, tq, ng):
    s = q.shape[0]
    g, _, n_slc, tsel = sel.shape
    _, nchunk, _, kc = vst.shape
    gw = NSA_GROUP * LANE
    sw = NSA_GROUP * tq
    resident = dict(pipeline_mode=pl.Buffered(1)) if g == ng else {}
    return pl.pallas_call(
        functools.partial(_selattn_body, tq=tq, kc=kc, ng=ng),
        grid=(g // ng, s // tq),
        in_specs=[
            pl.BlockSpec((tq, ng * gw), lambda a, i: (i, a)),
            pl.BlockSpec((s, ng * LANE), lambda a, i: (0, a), **resident),
            pl.BlockSpec((ng, nchunk, LANE, kc), lambda a, i: (a, 0, 0, 0), **resident),
            pl.BlockSpec((ng, tq // tsel, n_slc, tsel), lambda a, i: (a, i, 0, 0)),
        ],
        out_specs=pl.BlockSpec((tq, ng * gw), lambda a, i: (i, a)),
        out_shape=jax.ShapeDtypeStruct((s, g * gw), F32),
        scratch_shapes=[pltpu.VMEM((ng, 1, sw), F32), pltpu.VMEM((ng, 1, sw), F32),
                        pltpu.VMEM((ng, LANE, sw), F32)],
        compiler_params=_params("parallel", "arbitrary"),
        name="nsa_selattn",
    )(q, ks, vst, sel)


def _winattn_body(*refs, tq, nwb):
    q_ref = refs[0]
    k_refs = refs[1:2 + nwb]
    v_refs = refs[2 + nwb:3 + 2 * nwb]
    o_ref = refs[3 + 2 * nwb]
    qi = pl.program_id(1)
    q3 = _stack_heads(q_ref[...])
    k = jnp.concatenate([r[...] for r in k_refs], axis=0)
    v = jnp.concatenate([r[...] for r in v_refs], axis=0)
    s = _dot_nt(q3, k)
    tpos = qi * tq + (lax.broadcasted_iota(jnp.int32, s.shape, 0) & (tq - 1))
    kpos = (qi - nwb) * tq + lax.broadcasted_iota(jnp.int32, s.shape, 1)
    mask = (kpos >= 0) & (kpos <= tpos) & (tpos - kpos < WINDOW)
    s = jnp.where(mask, s, NEG)
    m = jnp.max(s, axis=1, keepdims=True)
    p = jnp.where(mask, jnp.exp(s - m), 0.0)
    l = jnp.sum(p, axis=1, keepdims=True)
    o3 = _dot(p.astype(BF16), v) / l
    o_ref[...] = _unstack_heads(o3).astype(o_ref.dtype)


def _winattn(q, kw, vw, *, tq):
    s = q.shape[0]
    g = kw.shape[1] // LANE
    nwb = WINDOW // tq
    gw = NSA_GROUP * LANE

    def kv_spec(back):
        return pl.BlockSpec((tq, LANE), lambda a, i: (jnp.maximum(i - back, 0), a))

    kv_specs = [kv_spec(nwb - b) for b in range(nwb + 1)]
    return pl.pallas_call(
        functools.partial(_winattn_body, tq=tq, nwb=nwb),
        grid=(g, s // tq),
        in_specs=[pl.BlockSpec((tq, gw), lambda a, i: (i, a))] + kv_specs + kv_specs,
        out_specs=pl.BlockSpec((tq, gw), lambda a, i: (i, a)),
        out_shape=jax.ShapeDtypeStruct((s, g * gw), F32),
        compiler_params=_params("parallel", "parallel"),
        name="nsa_winattn",
    )(q, *([kw] * (nwb + 1)), *([vw] * (nwb + 1)))


def _outproj_body(*refs, mode):
    if mode == "nsa":
        x_ref, oc_ref, os_ref, ow_ref, gl_ref, qx_ref, mk_ref, mv_ref, wmix_ref, wmem_ref, o_ref = refs
        gate = jax.nn.sigmoid(gl_ref[...])
        parts = []
        for hd in range(MIX_HEADS):
            sl = slice(hd * LANE, (hd + 1) * LANE)
            acc = None
            for b, br in enumerate((oc_ref, os_ref, ow_ref)):
                c = b * MIX_HEADS + hd
                term = gate[:, c:c + 1] * br[:, sl]
                acc = term if acc is None else acc + term
            parts.append(acc.astype(BF16))
        ymix = jnp.concatenate(parts, axis=1)
    else:
        x_ref, ymix_ref, qx_ref, mk_ref, mv_ref, wmix_ref, wmem_ref, o_ref = refs
        ymix = ymix_ref[...]
    y = _dot(ymix, wmix_ref[...])
    for hd in range(MEM_HEADS):
        sl = slice(hd * LANE, (hd + 1) * LANE)
        s = _dot_nt(qx_ref[:, sl], mk_ref[:, sl]) * SCALE
        m = jnp.max(s, axis=1, keepdims=True)
        p = jnp.exp(s - m)
        p = p / jnp.sum(p, axis=1, keepdims=True)
        ymem = _dot(p.astype(BF16), mv_ref[:, sl]).astype(BF16)
        y = y + _dot(ymem, wmem_ref[sl, :])
    o_ref[...] = x_ref[...] + y


def _outproj(x, mix_args, qx, mk, mv, wmix, wmem, *, mode, tm):
    s, d = x.shape
    row = lambda w: pl.BlockSpec((tm, w), lambda i: (i, 0))
    mix_specs = [row(a.shape[1]) for a in mix_args]
    return pl.pallas_call(
        functools.partial(_outproj_body, mode=mode),
        grid=(s // tm,),
        in_specs=[row(d)] + mix_specs + [row(qx.shape[1]), _const_spec(mk.shape), _const_spec(mv.shape),
                                         _const_spec(wmix.shape), _const_spec(wmem.shape)],
        out_specs=row(d),
        out_shape=jax.ShapeDtypeStruct((s, d), F32),
        compiler_params=_params("parallel"),
        name="outproj",
    )(x, *mix_args, qx, mk, mv, wmix, wmem)


def _pad_heads_cols(w, n_heads):
    d = w.shape[0]
    w = w.reshape(d, n_heads, HEAD_DIM)
    return jnp.pad(w, ((0, 0), (0, 0), (0, LANE - HEAD_DIM))).reshape(d, n_heads * LANE)


def _pad_heads_rows(w, n_heads):
    d = w.shape[1]
    w = w.reshape(n_heads, HEAD_DIM, d)
    return jnp.pad(w, ((0, 0), (0, LANE - HEAD_DIM), (0, 0))).reshape(n_heads * LANE, d)


def _pad_cols(w, width):
    return jnp.pad(w, ((0, 0), (0, width - w.shape[1])))


def _rope_tables(s):
    half = ROT_DIM // 2
    inv = ROPE_THETA ** (-jnp.arange(half, dtype=F32) / half)
    ang = jnp.arange(s, dtype=F32)[:, None] * inv[None, :]
    cos, sin = jnp.cos(ang), jnp.sin(ang)
    z = lambda n: jnp.zeros((s, n), F32)
    cos_t = jnp.concatenate([cos, cos, jnp.ones((s, LANE - ROT_DIM), F32)], axis=1)
    sa_t = jnp.concatenate([z(half), sin, z(LANE - ROT_DIM)], axis=1)
    sb_t = jnp.concatenate([-sin, z(LANE - half)], axis=1)
    return cos_t, sa_t, sb_t


def _pick(n, candidates):
    for c in candidates:
        if n % c == 0:
            return c
    raise ValueError(f"no tile size for {n}")


def kernel(x, mem, ffn1_norm, ffn1_w_in, ffn1_w_out, mix_norm, mix_w_out, mem_norm, mem_w_kv, fox_w_in,
           fox_b_f, conv_w_in, conv_w, nsa_w_in, nsa_cmp_pos, nsa_cmp_w1, nsa_cmp_w2, ffn2_norm, ffn2_w_in,
           ffn2_w_out, final_norm):
    b, s, d = x.shape
    assert b == 1
    depth = ffn1_norm.shape[0]
    mix_w = MIX_HEADS * HEAD_DIM
    mem_w = MEM_HEADS * HEAD_DIM
    kv_w = NSA_KV_HEADS * HEAD_DIM
    d_ff = ffn1_w_out.shape[1]
    tm_ffn = _pick(s, (1024, 512, 256, 128))
    fc = next(c for c in (d_ff // 4, d_ff // 2, 256, LANE) if d_ff % c == 0 and c % LANE == 0)
    tm = _pick(s, (512, 256, 128))
    tq_fox = _pick(s, (512, 256, 128))
    tq_nsa = 128
    kc_sel = _pick(s, (512, 256, 128))
    tq_win = _pick(s, (256, 128))
    m_len = mem.shape[1]

    xs = x[0]
    mem_s = mem[0]
    rope_t = None
    for i in range(depth):
        xs = _ffn(xs, ffn1_norm[i], ffn1_w_in[i].astype(BF16), ffn1_w_out[i].astype(BF16), tm=tm_ffn, fc=fc)

        wkv = mem_w_kv[i]
        wkv_p = jnp.concatenate([_pad_heads_cols(wkv[:, :mem_w], MEM_HEADS),
                                 _pad_heads_cols(wkv[:, mem_w:], MEM_HEADS)], axis=1).astype(BF16)
        mk, mv = _proj(mem_s, mem_norm, wkv_p, [(MEM_HEADS * LANE, BF16, False)] * 2, tm=m_len)

        w_out = mix_w_out[i]
        wmem = _pad_heads_rows(w_out[mix_w:], MEM_HEADS).astype(BF16)
        kind, j = i % N_MIXERS, i // N_MIXERS
        if kind == 0:
            w = fox_w_in[j]
            wq, wk, wv = (w[:, a * mix_w:(a + 1) * mix_w] for a in range(3))
            wf = w[:, 3 * mix_w:3 * mix_w + MIX_HEADS]
            wqx = w[:, 3 * mix_w + MIX_HEADS:]
            wp = jnp.concatenate([_pad_heads_cols(wq, MIX_HEADS), _pad_heads_cols(wk, MIX_HEADS),
                                  _pad_heads_cols(wv, MIX_HEADS), _pad_heads_cols(wqx, MEM_HEADS),
                                  _pad_cols(wf, LANE)], axis=1).astype(BF16)
            hw = MIX_HEADS * LANE
            q, k, v, qx, f = _proj(xs, mix_norm[i], wp,
                                   [(hw, BF16, False)] * 3 + [(MEM_HEADS * LANE, BF16, False), (LANE, F32, False)],
                                   tm=tm)
            ft = _fgate(f, _pad_cols(fox_b_f[j].reshape(1, MIX_HEADS), LANE), tk=tq_fox)
            ymix = _fox(q, k, v, ft, tq=tq_fox)
            wmix = _pad_heads_rows(w_out[:mix_w], MIX_HEADS).astype(BF16)
            xs = _outproj(xs, [ymix], qx, mk, mv, wmix, wmem, mode="plain", tm=tm)
        elif kind == 1:
            w = conv_w_in[j]
            wp = jnp.concatenate([w[:, :3 * mix_w], _pad_heads_cols(w[:, 3 * mix_w:], MEM_HEADS)],
                                 axis=1).astype(BF16)
            bg, cg, v, qx = _proj(xs, mix_norm[i], wp,
                                  [(mix_w, F32, False)] * 3 + [(MEM_HEADS * LANE, BF16, False)], tm=tm)
            ymix = _conv(bg, cg, v, conv_w[j], tm=tm)
            xs = _outproj(xs, [ymix], qx, mk, mv, w_out[:mix_w].astype(BF16), wmem, mode="plain", tm=tm)
        else:
            w = nsa_w_in[j]
            off = mix_w
            wq = w[:, :mix_w]
            kvs = []
            for _ in range(6):
                kvs.append(_pad_heads_cols(w[:, off:off + kv_w], NSA_KV_HEADS))
                off += kv_w
            wgl = w[:, off:off + 3 * MIX_HEADS]
            wqx = w[:, off + 3 * MIX_HEADS:]
            wp = jnp.concatenate([_pad_heads_cols(wq, MIX_HEADS)] + kvs +
                                 [_pad_heads_cols(wqx, MEM_HEADS), _pad_cols(wgl, LANE)], axis=1).astype(BF16)
            if rope_t is None:
                rope_t = _rope_tables(s)
            gwid = NSA_KV_HEADS * LANE
            segs = [(MIX_HEADS * LANE, BF16, True)]
            segs += [(gwid, BF16, a % 2 == 0) for a in range(6)]
            segs += [(MEM_HEADS * LANE, BF16, False), (LANE, F32, False)]
            q, kc, vc, ks_, vs_, kw, vw, qx, gl = _proj(xs, mix_norm[i], wp, segs, rope_t, tm=tm)

            n16 = s // CMP_STRIDE
            sub = CMP_STRIDE

            def to_t16(t):
                t = t.reshape(n16, sub, NSA_KV_HEADS, LANE)[..., :HEAD_DIM]
                return t.transpose(2, 0, 1, 3).reshape(NSA_KV_HEADS, n16, sub * HEAD_DIM)

            t16 = jnp.stack([to_t16(kc), to_t16(vc)])
            w1 = nsa_cmp_w1[j].astype(BF16)
            pe8 = jnp.zeros((2, SUBLANE, CMP_LEN * HEAD_DIM), F32).at[:, 0].set(
                nsa_cmp_pos[j].reshape(2, CMP_LEN * HEAD_DIM)).astype(BF16)
            w2d = jnp.pad(nsa_cmp_w2[j], ((0, 0), (0, 0), (0, LANE - HEAD_DIM))).astype(BF16)
            cmp_n, cmp_t = _cmp(t16, w1, pe8, w2d)

            n_slc = s // SLC_LEN
            n_top = min(SLC_TOPN, n_slc)
            jj = jnp.arange(n_slc)[:, None] * SLC_LEN
            nn = jnp.arange(n16)[None, :] * CMP_STRIDE
            ov = ((nn < jj + SLC_LEN) & (nn + CMP_LEN > jj) & (nn + CMP_LEN <= s)).astype(BF16)
            oc, sel = _cmpattn(q, cmp_n, cmp_t, ov, tq=tq_nsa, n_top=n_top)

            vst = vs_.reshape(s // kc_sel, kc_sel, NSA_KV_HEADS, LANE).transpose(2, 0, 3, 1)
            osel = _selattn(q, ks_, vst, sel, tq=2 * tq_nsa, ng=NSA_KV_HEADS)
            ow = _winattn(q, kw, vw, tq=tq_win)
            wmix = _pad_heads_rows(w_out[:mix_w], MIX_HEADS).astype(BF16)
            xs = _outproj(xs, [oc, osel, ow, gl], qx, mk, mv, wmix, wmem, mode="nsa", tm=tm)

        last = i == depth - 1
        xs = _ffn(xs, ffn2_norm[i], ffn2_w_in[i].astype(BF16), ffn2_w_out[i].astype(BF16),
                  final_norm if last else None, tm=tm_ffn, fc=fc)
    return xs[None]
```

```python
import functools
import math

import jax
import jax.numpy as jnp
from jax import lax
from jax.experimental import pallas as pl
from jax.experimental.pallas import tpu as pltpu

F32 = jnp.float32
BF16 = jnp.bfloat16

HEAD_DIM = 64
MIX_HEADS = 12
MEM_HEADS = 4
N_MIXERS = 3
ROT_DIM = HEAD_DIM // 4
ROPE_THETA = 500000.0
CONV_WIDTH = 3
NSA_KV_HEADS = 4
NSA_GROUP = MIX_HEADS // NSA_KV_HEADS
CMP_LEN = 32
CMP_STRIDE = 16
SLC_LEN = 64
SLC_TOPN = 16
WINDOW = 512
EPS = 1e-6

LANE = 128
SUBLANE = 8
VMEM_LIMIT_BYTES = 56 * 2**20
NEG = -1e30
SCALE = 1.0 / math.sqrt(HEAD_DIM)


def _params(*sem):
    return pltpu.CompilerParams(dimension_semantics=sem, vmem_limit_bytes=VMEM_LIMIT_BYTES)


def _const_spec(shape):
    zeros = (0,) * len(shape)
    return pl.BlockSpec(shape, lambda *_: zeros, pipeline_mode=pl.Buffered(1))


def _rms_bf16(x, g):
    ms = jnp.mean(x * x, axis=-1, keepdims=True)
    return (x * lax.rsqrt(ms + EPS) * g).astype(BF16)


def _dot(a, b):
    return jnp.dot(a, b, preferred_element_type=F32)


def _dot_nt(a, b):
    return lax.dot_general(a, b, (((1,), (1,)), ((), ())), preferred_element_type=F32)


def _ffn_body(x_ref, g_ref, wg_ref, wu_ref, wo_ref, fg_ref, o_ref, hn_ref, acc_ref, *, final_norm):
    j = pl.program_id(1)

    @pl.when(j == 0)
    def _():
        hn_ref[...] = _rms_bf16(x_ref[...], g_ref[...])

    hn = hn_ref[...]
    gate = _dot(hn, wg_ref[...])
    up = _dot(hn, wu_ref[...])
    act = (gate * jax.nn.sigmoid(gate) * up).astype(BF16)
    part = _dot(act, wo_ref[...])

    @pl.when(j == 0)
    def _():
        acc_ref[...] = part

    @pl.when(j > 0)
    def _():
        acc_ref[...] += part

    @pl.when(j == pl.num_programs(1) - 1)
    def _():
        y = x_ref[...] + 0.5 * acc_ref[...]
        if final_norm:
            ms = jnp.mean(y * y, axis=-1, keepdims=True)
            y = y * lax.rsqrt(ms + EPS) * fg_ref[...]
        o_ref[...] = y


def _ffn(x, g, w_in, w_out, final_g=None, *, tm, fc):
    s, d = x.shape
    f = w_out.shape[0]
    nf = f // fc
    fg = jnp.ones((1, d), F32) if final_g is None else final_g.reshape(1, d)
    return pl.pallas_call(
        functools.partial(_ffn_body, final_norm=final_g is not None),
        grid=(s // tm, nf),
        in_specs=[
            pl.BlockSpec((tm, d), lambda i, j: (i, 0)),
            pl.BlockSpec((1, d), lambda i, j: (0, 0)),
            pl.BlockSpec((d, fc), lambda i, j: (0, j)),
            pl.BlockSpec((d, fc), lambda i, j: (0, nf + j)),
            pl.BlockSpec((fc, d), lambda i, j: (j, 0)),
            pl.BlockSpec((1, d), lambda i, j: (0, 0)),
        ],
        out_specs=pl.BlockSpec((tm, d), lambda i, j: (i, 0)),
        out_shape=jax.ShapeDtypeStruct((s, d), F32),
        scratch_shapes=[pltpu.VMEM((tm, d), BF16), pltpu.VMEM((tm, d), F32)],
        compiler_params=_params("parallel", "arbitrary"),
        name="ffn",
    )(x, g.reshape(1, d), w_in, w_in, w_out, fg)


def _proj_body(*refs, segs, has_rope, chunk):
    n_in = 6 if has_rope else 3
    x_ref, g_ref, w_ref = refs[:3]
    outs = refs[n_in:]
    hn = _rms_bf16(x_ref[...], g_ref[...])
    if has_rope:
        cos_t, sa_t, sb_t = (r[...] for r in refs[3:6])
    col = 0
    for o_ref, (width, rope) in zip(outs, segs):
        for c0 in range(0, width, chunk):
            cw = min(chunk, width - c0)
            y = _dot(hn, w_ref[:, col + c0:col + c0 + cw])
            if rope:
                rep = cw // LANE
                cos_c, sa_c, sb_c = (jnp.concatenate([t] * rep, axis=1) if rep > 1 else t
                                     for t in (cos_t, sa_t, sb_t))
                half = ROT_DIM // 2
                y = y * cos_c + pltpu.roll(y, half, axis=1) * sa_c + pltpu.roll(y, cw - half, axis=1) * sb_c
            o_ref[:, c0:c0 + cw] = y.astype(o_ref.dtype)
        col += width


def _proj(x, g, w, segs, rope_tables=None, *, tm, chunk=256):
    s, d = x.shape
    n = w.shape[1]
    assert n == sum(sg[0] for sg in segs)
    has_rope = rope_tables is not None
    in_specs = [
        pl.BlockSpec((tm, d), lambda i: (i, 0)),
        pl.BlockSpec((1, d), lambda i: (0, 0)),
        _const_spec((d, n)),
    ]
    args = [x, g.reshape(1, d), w]
    if has_rope:
        in_specs += [pl.BlockSpec((tm, LANE), lambda i: (i, 0))] * 3
        args += list(rope_tables)
    return pl.pallas_call(
        functools.partial(_proj_body, segs=tuple((sg[0], sg[2]) for sg in segs), has_rope=has_rope,
                          chunk=chunk),
        grid=(s // tm,),
        in_specs=in_specs,
        out_specs=[pl.BlockSpec((tm, sg[0]), lambda i: (i, 0)) for sg in segs],
        out_shape=[jax.ShapeDtypeStruct((s, sg[0]), sg[1]) for sg in segs],
        compiler_params=_params("parallel"),
        name="proj",
    )(*args)


def _fgate_body(f_ref, b_ref, o_ref, carry_ref):
    i = pl.program_id(0)

    @pl.when(i == 0)
    def _():
        carry_ref[...] = jnp.zeros_like(carry_ref)

    z = f_ref[...] + b_ref[...]
    x = jnp.minimum(z, 0.0) - jnp.log(1.0 + jnp.exp(-jnp.abs(z)))
    tm = x.shape[0]
    row = lax.broadcasted_iota(jnp.int32, x.shape, 0)
    sh = 1
    while sh < tm:
        x = x + jnp.where(row >= sh, pltpu.roll(x, sh, axis=0), 0.0)
        sh *= 2
    x = x + carry_ref[...]
    carry_ref[...] = x[tm - 1:tm, :]
    o_ref[0] = x.T[:2 * SUBLANE, :]


def _fgate(f, b, *, tk):
    s = f.shape[0]
    return pl.pallas_call(
        _fgate_body,
        grid=(s // tk,),
        in_specs=[pl.BlockSpec((tk, LANE), lambda i: (i, 0)), pl.BlockSpec((1, LANE), lambda i: (0, 0))],
        out_specs=pl.BlockSpec((1, 2 * SUBLANE, tk), lambda i: (i, 0, 0)),
        out_shape=jax.ShapeDtypeStruct((s // tk, 2 * SUBLANE, tk), F32),
        scratch_shapes=[pltpu.VMEM((1, LANE), F32)],
        compiler_params=_params("arbitrary"),
        name="fgate",
    )(f, b)


UNDERFLOW = 110.0


def _fox_body(q_ref, k_ref, v_ref, ft_ref, fend_ref, o_ref, m_ref, l_ref, acc_ref, kmax2_ref, *, tq, nhs):
    hp = pl.program_id(0)
    qi = pl.program_id(1)
    m_ref[...] = jnp.full_like(m_ref, NEG)
    l_ref[...] = jnp.zeros_like(l_ref)
    acc_ref[...] = jnp.zeros_like(acc_ref)
    cols = [slice(b * LANE, (b + 1) * LANE) for b in range(nhs)]

    @pl.when(qi == 0)
    def _():
        for b in range(nhs):
            def kmax_body(c, mx, b=b):
                kf = k_ref[pl.ds(pl.multiple_of(c * tq, tq), tq), cols[b]].astype(F32)
                return jnp.maximum(mx, jnp.sum(kf * kf, axis=1, keepdims=True))

            mx = lax.fori_loop(0, k_ref.shape[0] // tq, kmax_body, jnp.zeros((tq, 1), F32))
            kmax2_ref[b] = jnp.max(mx, axis=0, keepdims=True)

    qs, f0s, kb_lo = [], [], qi
    for b in range(nhs):
        h = hp * nhs + b
        q = q_ref[:, cols[b]] * SCALE
        f0 = ft_ref[qi, pl.ds(h, 1), :][:, 0:1]
        qf = q.astype(F32)
        qmax2 = jnp.max(jnp.sum(qf * qf, axis=1, keepdims=True), axis=0, keepdims=True)
        bound = f0 + (UNDERFLOW + 2.0 * jnp.sqrt(qmax2 * kmax2_ref[b]))
        fend = fend_ref[pl.ds(h, 1), :]
        blk = lax.broadcasted_iota(jnp.int32, fend.shape, 1)
        kb_lo = jnp.minimum(kb_lo, jnp.sum(jnp.where((fend > bound) & (blk < qi), 1, 0)))
        qs.append(q)
        f0s.append(f0)

    def step(kb, causal):
        start = pl.multiple_of(kb * tq, tq)
        for b in range(nhs):
            h = hp * nhs + b
            k = k_ref[pl.ds(start, tq), cols[b]]
            v = v_ref[pl.ds(start, tq), cols[b]]
            s = _dot_nt(qs[b], k) + (f0s[b] - ft_ref[kb, pl.ds(h, 1), :])
            if causal:
                r = lax.broadcasted_iota(jnp.int32, s.shape, 0)
                c = lax.broadcasted_iota(jnp.int32, s.shape, 1)
                s = jnp.where(c <= r, s, NEG)
            m_old = m_ref[b]
            m_new = jnp.maximum(m_old, jnp.max(s, axis=1, keepdims=True))
            alpha = jnp.exp(m_old - m_new)
            p = jnp.exp(s - m_new)
            l_ref[b] = alpha * l_ref[b] + jnp.sum(p, axis=1, keepdims=True)
            acc_ref[b] = alpha * acc_ref[b] + _dot(p.astype(BF16), v)
            m_ref[b] = m_new

    def loop_body(kb, carry):
        step(kb, False)
        return carry

    lax.fori_loop(kb_lo, qi, loop_body, 0)
    step(qi, True)
    for b in range(nhs):
        o_ref[:, cols[b]] = (acc_ref[b] / l_ref[b]).astype(o_ref.dtype)


def _fox(q, k, v, ft, *, tq, nhs=1):
    s = q.shape[0]
    nh = q.shape[1] // LANE
    nq = s // tq
    assert nq <= LANE and nh % nhs == 0
    w = nhs * LANE
    fend = jnp.pad(ft[:, :, tq - 1].T, ((0, 0), (0, LANE - nq)))
    return pl.pallas_call(
        functools.partial(_fox_body, tq=tq, nhs=nhs),
        grid=(nh // nhs, nq),
        in_specs=[
            pl.BlockSpec((tq, w), lambda h, i: (i, h)),
            pl.BlockSpec((s, w), lambda h, i: (0, h)),
            pl.BlockSpec((s, w), lambda h, i: (0, h)),
            _const_spec(ft.shape),
            _const_spec(fend.shape),
        ],
        out_specs=pl.BlockSpec((tq, w), lambda h, i: (i, h)),
        out_shape=jax.ShapeDtypeStruct((s, nh * LANE), BF16),
        scratch_shapes=[pltpu.VMEM((nhs, tq, 1), F32), pltpu.VMEM((nhs, tq, 1), F32),
                        pltpu.VMEM((nhs, tq, LANE), F32), pltpu.VMEM((nhs, 1, 1), F32)],
        compiler_params=_params("arbitrary", "arbitrary"),
        name="fox",
    )(q, k, v, ft, fend)


def _conv_body(bg_ref, cg_ref, v_ref, cgh_ref, vh_ref, w_ref, o_ref):
    i = pl.program_id(0)
    u = cg_ref[...] * v_ref[...]
    halo = jnp.where(i > 0, cgh_ref[...] * vh_ref[...], 0.0)
    ext = jnp.concatenate([halo, u], axis=0)
    u1 = pltpu.roll(ext, 1, axis=0)[SUBLANE:, :]
    u2 = pltpu.roll(ext, 2, axis=0)[SUBLANE:, :]
    w = w_ref[...]
    y = w[0:1, :] * u2 + w[1:2, :] * u1 + w[2:3, :] * u
    o_ref[...] = (bg_ref[...] * y).astype(o_ref.dtype)


def _conv(bg, cg, v, w, *, tm):
    s, c = bg.shape
    w8 = jnp.zeros((SUBLANE, c), F32).at[:CONV_WIDTH].set(w)
    per = tm // SUBLANE
    main = pl.BlockSpec((tm, c), lambda i: (i, 0))
    halo = pl.BlockSpec((SUBLANE, c), lambda i: (jnp.maximum(i * per - 1, 0), 0))
    return pl.pallas_call(
        _conv_body,
        grid=(s // tm,),
        in_specs=[main, main, main, halo, halo, pl.BlockSpec((SUBLANE, c), lambda i: (0, 0))],
        out_specs=main,
        out_shape=jax.ShapeDtypeStruct((s, c), BF16),
        compiler_params=_params("parallel"),
        name="conv",
    )(bg, cg, v, cg, v, w8)


def _cmp_body(t_ref, w1_ref, pe_ref, w2_ref, on_ref, ot_ref):
    t = t_ref[0, 0]
    n16 = t.shape[0]
    half = w1_ref.shape[1] // 2
    a = _dot(t, w1_ref[0, :half, :])
    b = _dot(t, w1_ref[0, half:, :])
    pec = _dot(pe_ref[0], w1_ref[0])[0:1, :]
    pre = a + pltpu.roll(b, n16 - 1, axis=0) + pec
    hid = (pre * jax.nn.sigmoid(pre)).astype(BF16)
    out = _dot(hid, w2_ref[0])
    on_ref[0, 0] = out.astype(on_ref.dtype)
    ot_ref[0, 0] = out.T.astype(ot_ref.dtype)


def _cmp(t16, w1, pe8, w2d):
    _, g, n16, kk = t16.shape
    return pl.pallas_call(
        _cmp_body,
        grid=(2, g),
        in_specs=[
            pl.BlockSpec((1, 1, n16, kk), lambda a, b: (a, b, 0, 0)),
            pl.BlockSpec((1, 2 * kk, HEAD_DIM), lambda a, b: (a, 0, 0)),
            pl.BlockSpec((1, SUBLANE, 2 * kk), lambda a, b: (a, 0, 0)),
            pl.BlockSpec((1, HEAD_DIM, LANE), lambda a, b: (a, 0, 0)),
        ],
        out_specs=[
            pl.BlockSpec((1, 1, n16, LANE), lambda a, b: (a, b, 0, 0)),
            pl.BlockSpec((1, 1, LANE, n16), lambda a, b: (a, b, 0, 0)),
        ],
        out_shape=[
            jax.ShapeDtypeStruct((2, g, n16, LANE), BF16),
            jax.ShapeDtypeStruct((2, g, LANE, n16), BF16),
        ],
        compiler_params=_params("parallel", "parallel"),
        name="nsa_cmp",
    )(t16, w1, pe8, w2d)


def _stack_heads(q_blk):
    return jnp.concatenate([q_blk[:, j * LANE:(j + 1) * LANE] for j in range(NSA_GROUP)], axis=0) * SCALE


def _unstack_heads(o3):
    tq = o3.shape[0] // NSA_GROUP
    return jnp.concatenate([o3[j * tq:(j + 1) * tq, :] for j in range(NSA_GROUP)], axis=1)


def _cmpattn_body(q_ref, kc_ref, vct_ref, ov_ref, oc_ref, sel_ref, imp_ref, *, tq, n_top, n_bucket):
    qi = pl.program_id(1)
    nq = pl.num_programs(1)
    t0 = qi * tq
    q3 = _stack_heads(q_ref[...])
    n16 = kc_ref.shape[2]

    def attend(rows):
        st = _dot_nt(kc_ref[0, 0, :rows, :], q3)
        n_idx = lax.broadcasted_iota(jnp.int32, st.shape, 0)
        lane = lax.broadcasted_iota(jnp.int32, st.shape, 1)
        tpos = t0 + (lane & (tq - 1))
        mask = n_idx * CMP_STRIDE + (CMP_LEN - 1) <= tpos
        st = jnp.where(mask, st, NEG)
        m = jnp.max(st, axis=0, keepdims=True)
        p = jnp.where(mask, jnp.exp(st - m), 0.0)
        l = jnp.sum(p, axis=0, keepdims=True)
        inv_l = 1.0 / jnp.maximum(l, jnp.finfo(F32).tiny)
        pn = p * inv_l
        oct_ = _dot(vct_ref[0, 0, :, :rows], p.astype(BF16)) * inv_l
        oc_ref[...] = _unstack_heads(oct_.T).astype(oc_ref.dtype)
        psum = pn[:, 0:tq]
        for j in range(1, NSA_GROUP):
            psum = psum + pn[:, j * tq:(j + 1) * tq]
        p_hi = psum.astype(BF16)
        p_lo = (psum - p_hi.astype(F32)).astype(BF16)
        ov = ov_ref[:, :rows]
        imp_ref[...] = _dot(ov, p_hi) + _dot(ov, p_lo)

    for k in range(n_bucket):
        @pl.when((qi * n_bucket >= k * nq) & (qi * n_bucket < (k + 1) * nq))
        def _(k=k):
            attend((k + 1) * n16 // n_bucket)

    imp = imp_ref[...]
    jf = lax.broadcasted_iota(jnp.int32, imp.shape, 0).astype(F32)
    tl = t0 + lax.broadcasted_iota(jnp.int32, imp.shape, 1)
    tblk = jnp.right_shift(tl, SLC_LEN.bit_length() - 1).astype(F32)
    valid = jf <= tblk
    forced = (jf == 0.0) | (jf == tblk) | (jf == tblk - 1.0)
    sel = jnp.where(valid & forced, 1.0, 0.0)
    score = jnp.where(valid & jnp.logical_not(forced), imp, -jnp.inf)
    big = float(imp.shape[0])
    for _ in range(max(n_top - 3, 0)):
        mx = jnp.max(score, axis=0, keepdims=True)
        idx = jnp.min(jnp.where(score == mx, jf, big), axis=0, keepdims=True)
        hit = jf == idx
        sel = jnp.where(hit & (mx > -jnp.inf), 1.0, sel)
        score = jnp.where(hit, -jnp.inf, score)
    sel_ref[0, 0] = sel


def _cmpattn(q, kcmp, vcmpt, ov, *, tq, n_top):
    s = q.shape[0]
    g, n16 = kcmp.shape[1], kcmp.shape[2]
    n_slc = ov.shape[0]
    nq = s // tq
    gw = NSA_GROUP * LANE
    n_bucket = 4 if n16 % (4 * LANE) == 0 and nq % 4 == 0 else 1
    return pl.pallas_call(
        functools.partial(_cmpattn_body, tq=tq, n_top=n_top, n_bucket=n_bucket),
        grid=(g, nq),
        in_specs=[
            pl.BlockSpec((tq, gw), lambda a, i: (i, a)),
            pl.BlockSpec((1, 1, n16, LANE), lambda a, i: (0, a, 0, 0)),
            pl.BlockSpec((1, 1, LANE, n16), lambda a, i: (1, a, 0, 0)),
            _const_spec(ov.shape),
        ],
        out_specs=[
            pl.BlockSpec((tq, gw), lambda a, i: (i, a)),
            pl.BlockSpec((1, 1, n_slc, tq), lambda a, i: (a, i, 0, 0)),
        ],
        out_shape=[
            jax.ShapeDtypeStruct((s, g * gw), F32),
            jax.ShapeDtypeStruct((g, nq, n_slc, tq), F32),
        ],
        scratch_shapes=[pltpu.VMEM((n_slc, tq), F32)],
        compiler_params=_params("parallel", "arbitrary"),
        name="nsa_cmpattn",
    )(q, kcmp, vcmpt, ov)


def _selattn_body(q_ref, k_ref, vt_ref, sel_ref, o_ref, m_ref, l_ref, acc_ref, *, tq, kc, ng):
    qi = pl.program_id(1)
    gw = NSA_GROUP * LANE
    q3s = [_stack_heads(q_ref[:, b * gw:(b + 1) * gw]) for b in range(ng)]
    m_ref[...] = jnp.full_like(m_ref, NEG)
    l_ref[...] = jnp.zeros_like(l_ref)
    acc_ref[...] = jnp.zeros_like(acc_ref)
    bpc = kc // SLC_LEN
    n_full = (qi * tq) // kc

    def chunk(c, causal):
        start = pl.multiple_of(c * kc, kc)
        for b in range(ng):
            k = k_ref[pl.ds(start, kc), b * LANE:(b + 1) * LANE]
            st = _dot_nt(k, q3s[b])
            blk_rows = pl.ds(pl.multiple_of(c * bpc, bpc), bpc)
            srows = [sel_ref[b, u, blk_rows, :] for u in range(tq // sel_ref.shape[3])]
            pen = (jnp.concatenate(srows * NSA_GROUP, axis=1) - 1.0) * (-NEG)
            st = st + jnp.concatenate(
                [jnp.broadcast_to(pen[r:r + 1, :], (SLC_LEN, pen.shape[1])) for r in range(bpc)], axis=0)
            if causal:
                kpos = c * kc + lax.broadcasted_iota(jnp.int32, st.shape, 0)
                tpos = qi * tq + (lax.broadcasted_iota(jnp.int32, st.shape, 1) & (tq - 1))
                st = jnp.where(kpos <= tpos, st, NEG)
            m_old = m_ref[b]
            m_new = jnp.maximum(m_old, jnp.max(st, axis=0, keepdims=True))
            alpha = jnp.exp(m_old - m_new)
            p = jnp.exp(st - m_new)
            l_ref[b] = alpha * l_ref[b] + jnp.sum(p, axis=0, keepdims=True)
            acc_ref[b] = alpha * acc_ref[b] + _dot(vt_ref[b, c], p.astype(BF16))
            m_ref[b] = m_new

    def loop_body(c, carry):
        chunk(c, False)
        return carry

    lax.fori_loop(0, n_full, loop_body, 0)
    chunk(n_full, True)
    for b in range(ng):
        o_ref[:, b * gw:(b + 1) * gw] = _unstack_heads((acc_ref[b] / l_ref[b]).T).astype(o_ref.dtype)


def _selattn(q, ks, vst, sel, *, tq, ng):
    s = q.shape[0]
    g, _, n_slc, tsel = sel.shape
    _, nchunk, _, kc = vst.shape
    gw = NSA_GROUP * LANE
    sw = NSA_GROUP * tq
    resident = dict(pipeline_mode=pl.Buffered(1)) if g == ng else {}
    return pl.pallas_call(
        functools.partial(_selattn_body, tq=tq, kc=kc, ng=ng),
        grid=(g // ng, s // tq),
        in_specs=[
            pl.BlockSpec((tq, ng * gw), lambda a, i: (i, a)),
            pl.BlockSpec((s, ng * LANE), lambda a, i: (0, a), **resident),
            pl.BlockSpec((ng, nchunk, LANE, kc), lambda a, i: (a, 0, 0, 0), **resident),
            pl.BlockSpec((ng, tq // tsel, n_slc, tsel), lambda a, i: (a, i, 0, 0)),
        ],
        out_specs=pl.BlockSpec((tq, ng * gw), lambda a, i: (i, a)),
        out_shape=jax.ShapeDtypeStruct((s, g * gw), F32),
        scratch_shapes=[pltpu.VMEM((ng, 1, sw), F32), pltpu.VMEM((ng, 1, sw), F32),
                        pltpu.VMEM((ng, LANE, sw), F32)],
        compiler_params=_params("parallel", "arbitrary"),
        name="nsa_selattn",
    )(q, ks, vst, sel)


def _winattn_body(*refs, tq, nwb):
    q_ref = refs[0]
    k_refs = refs[1:2 + nwb]
    v_refs = refs[2 + nwb:3 + 2 * nwb]
    o_ref = refs[3 + 2 * nwb]
    qi = pl.program_id(1)
    q3 = _stack_heads(q_ref[...])
    k = jnp.concatenate([r[...] for r in k_refs], axis=0)
    v = jnp.concatenate([r[...] for r in v_refs], axis=0)
    s = _dot_nt(q3, k)
    tpos = qi * tq + (lax.broadcasted_iota(jnp.int32, s.shape, 0) & (tq - 1))
    kpos = (qi - nwb) * tq + lax.broadcasted_iota(jnp.int32, s.shape, 1)
    mask = (kpos >= 0) & (kpos <= tpos) & (tpos - kpos < WINDOW)
    s = jnp.where(mask, s, NEG)
    m = jnp.max(s, axis=1, keepdims=True)
    p = jnp.where(mask, jnp.exp(s - m), 0.0)
    l = jnp.sum(p, axis=1, keepdims=True)
    o3 = _dot(p.astype(BF16), v) / l
    o_ref[...] = _unstack_heads(o3).astype(o_ref.dtype)


def _winattn(q, kw, vw, *, tq):
    s = q.shape[0]
    g = kw.shape[1] // LANE
    nwb = WINDOW // tq
    gw = NSA_GROUP * LANE

    def kv_spec(back):
        return pl.BlockSpec((tq, LANE), lambda a, i: (jnp.maximum(i - back, 0), a))

    kv_specs = [kv_spec(nwb - b) for b in range(nwb + 1)]
    return pl.pallas_call(
        functools.partial(_winattn_body, tq=tq, nwb=nwb),
        grid=(g, s // tq),
        in_specs=[pl.BlockSpec((tq, gw), lambda a, i: (i, a))] + kv_specs + kv_specs,
        out_specs=pl.BlockSpec((tq, gw), lambda a, i: (i, a)),
        out_shape=jax.ShapeDtypeStruct((s, g * gw), F32),
        compiler_params=_params("parallel", "parallel"),
        name="nsa_winattn",
    )(q, *([kw] * (nwb + 1)), *([vw] * (nwb + 1)))


def _outproj_body(*refs, mode):
    if mode == "nsa":
        x_ref, oc_ref, os_ref, ow_ref, gl_ref, qx_ref, mk_ref, mv_ref, wmix_ref, wmem_ref, o_ref = refs
        gate = jax.nn.sigmoid(gl_ref[...])
        parts = []
        for hd in range(MIX_HEADS):
            sl = slice(hd * LANE, (hd + 1) * LANE)
            acc = None
            for b, br in enumerate((oc_ref, os_ref, ow_ref)):
                c = b * MIX_HEADS + hd
                term = gate[:, c:c + 1] * br[:, sl]
                acc = term if acc is None else acc + term
            parts.append(acc.astype(BF16))
        ymix = jnp.concatenate(parts, axis=1)
    else:
        x_ref, ymix_ref, qx_ref, mk_ref, mv_ref, wmix_ref, wmem_ref, o_ref = refs
        ymix = ymix_ref[...]
    y = _dot(ymix, wmix_ref[...])
    for hd in range(MEM_HEADS):
        sl = slice(hd * LANE, (hd + 1) * LANE)
        s = _dot_nt(qx_ref[:, sl], mk_ref[:, sl]) * SCALE
        m = jnp.max(s, axis=1, keepdims=True)
        p = jnp.exp(s - m)
        p = p / jnp.sum(p, axis=1, keepdims=True)
        ymem = _dot(p.astype(BF16), mv_ref[:, sl]).astype(BF16)
        y = y + _dot(ymem, wmem_ref[sl, :])
    o_ref[...] = x_ref[...] + y


def _outproj(x, mix_args, qx, mk, mv, wmix, wmem, *, mode, tm):
    s, d = x.shape
    row = lambda w: pl.BlockSpec((tm, w), lambda i: (i, 0))
    mix_specs = [row(a.shape[1]) for a in mix_args]
    return pl.pallas_call(
        functools.partial(_outproj_body, mode=mode),
        grid=(s // tm,),
        in_specs=[row(d)] + mix_specs + [row(qx.shape[1]), _const_spec(mk.shape), _const_spec(mv.shape),
                                         _const_spec(wmix.shape), _const_spec(wmem.shape)],
        out_specs=row(d),
        out_shape=jax.ShapeDtypeStruct((s, d), F32),
        compiler_params=_params("parallel"),
        name="outproj",
    )(x, *mix_args, qx, mk, mv, wmix, wmem)


def _pad_heads_cols(w, n_heads):
    d = w.shape[0]
    w = w.reshape(d, n_heads, HEAD_DIM)
    return jnp.pad(w, ((0, 0), (0, 0), (0, LANE - HEAD_DIM))).reshape(d, n_heads * LANE)


def _pad_heads_rows(w, n_heads):
    d = w.shape[1]
    w = w.reshape(n_heads, HEAD_DIM, d)
    return jnp.pad(w, ((0, 0), (0, LANE - HEAD_DIM), (0, 0))).reshape(n_heads * LANE, d)


def _pad_cols(w, width):
    return jnp.pad(w, ((0, 0), (0, width - w.shape[1])))


def _rope_tables(s):
    half = ROT_DIM // 2
    inv = ROPE_THETA ** (-jnp.arange(half, dtype=F32) / half)
    ang = jnp.arange(s, dtype=F32)[:, None] * inv[None, :]
    cos, sin = jnp.cos(ang), jnp.sin(ang)
    z = lambda n: jnp.zeros((s, n), F32)
    cos_t = jnp.concatenate([cos, cos, jnp.ones((s, LANE - ROT_DIM), F32)], axis=1)
    sa_t = jnp.concatenate([z(half), sin, z(LANE - ROT_DIM)], axis=1)
    sb_t = jnp.concatenate([-sin, z(LANE - half)], axis=1)
    return cos_t, sa_t, sb_t


def _pick(n, candidates):
    for c in candidates:
        if n % c == 0:
            return c
    raise ValueError(f"no tile size for {n}")


def kernel(x, mem, ffn1_norm, ffn1_w_in, ffn1_w_out, mix_norm, mix_w_out, mem_norm, mem_w_kv, fox_w_in,
           fox_b_f, conv_w_in, conv_w, nsa_w_in, nsa_cmp_pos, nsa_cmp_w1, nsa_cmp_w2, ffn2_norm, ffn2_w_in,
           ffn2_w_out, final_norm):
    b, s, d = x.shape
    assert b == 1
    depth = ffn1_norm.shape[0]
    mix_w = MIX_HEADS * HEAD_DIM
    mem_w = MEM_HEADS * HEAD_DIM
    kv_w = NSA_KV_HEADS * HEAD_DIM
    d_ff = ffn1_w_out.shape[1]
    tm_ffn = _pick(s, (1024, 512, 256, 128))
    fc = next(c for c in (d_ff // 4, d_ff // 2, 256, LANE) if d_ff % c == 0 and c % LANE == 0)
    tm = _pick(s, (512, 256, 128))
    tq_fox = _pick(s, (512, 256, 128))
    tq_nsa = 128
    kc_sel = _pick(s, (512, 256, 128))
    tq_win = _pick(s, (256, 128))
    m_len = mem.shape[1]

    xs = x[0]
    mem_s = mem[0]
    rope_t = None
    for i in range(depth):
        xs = _ffn(xs, ffn1_norm[i], ffn1_w_in[i].astype(BF16), ffn1_w_out[i].astype(BF16), tm=tm_ffn, fc=fc)

        wkv = mem_w_kv[i]
        wkv_p = jnp.concatenate([_pad_heads_cols(wkv[:, :mem_w], MEM_HEADS),
                                 _pad_heads_cols(wkv[:, mem_w:], MEM_HEADS)], axis=1).astype(BF16)
        mk, mv = _proj(mem_s, mem_norm, wkv_p, [(MEM_HEADS * LANE, BF16, False)] * 2, tm=m_len)

        w_out = mix_w_out[i]
        wmem = _pad_heads_rows(w_out[mix_w:], MEM_HEADS).astype(BF16)
        kind, j = i % N_MIXERS, i // N_MIXERS
        if kind == 0:
            w = fox_w_in[j]
            wq, wk, wv = (w[:, a * mix_w:(a + 1) * mix_w] for a in range(3))
            wf = w[:, 3 * mix_w:3 * mix_w + MIX_HEADS]
            wqx = w[:, 3 * mix_w + MIX_HEADS:]
            wp = jnp.concatenate([_pad_heads_cols(wq, MIX_HEADS), _pad_heads_cols(wk, MIX_HEADS),
                                  _pad_heads_cols(wv, MIX_HEADS), _pad_heads_cols(wqx, MEM_HEADS),
                                  _pad_cols(wf, LANE)], axis=1).astype(BF16)
            hw = MIX_HEADS * LANE
            q, k, v, qx, f = _proj(xs, mix_norm[i], wp,
                                   [(hw, BF16, False)] * 3 + [(MEM_HEADS * LANE, BF16, False), (LANE, F32, False)],
                                   tm=tm)
            ft = _fgate(f, _pad_cols(fox_b_f[j].reshape(1, MIX_HEADS), LANE), tk=tq_fox)
            ymix = _fox(q, k, v, ft, tq=tq_fox)
            wmix = _pad_heads_rows(w_out[:mix_w], MIX_HEADS).astype(BF16)
            xs = _outproj(xs, [ymix], qx, mk, mv, wmix, wmem, mode="plain", tm=tm)
        elif kind == 1:
            w = conv_w_in[j]
            wp = jnp.concatenate([w[:, :3 * mix_w], _pad_heads_cols(w[:, 3 * mix_w:], MEM_HEADS)],
                                 axis=1).astype(BF16)
            bg, cg, v, qx = _proj(xs, mix_norm[i], wp,
                                  [(mix_w, F32, False)] * 3 + [(MEM_HEADS * LANE, BF16, False)], tm=tm)
            ymix = _conv(bg, cg, v, conv_w[j], tm=tm)
            xs = _outproj(xs, [ymix], qx, mk, mv, w_out[:mix_w].astype(BF16), wmem, mode="plain", tm=tm)
        else:
            w = nsa_w_in[j]
            off = mix_w
            wq = w[:, :mix_w]
            kvs = []
            for _ in range(6):
                kvs.append(_pad_heads_cols(w[:, off:off + kv_w], NSA_KV_HEADS))
                off += kv_w
            wgl = w[:, off:off + 3 * MIX_HEADS]
            wqx = w[:, off + 3 * MIX_HEADS:]
            wp = jnp.concatenate([_pad_heads_cols(wq, MIX_HEADS)] + kvs +
                                 [_pad_heads_cols(wqx, MEM_HEADS), _pad_cols(wgl, LANE)], axis=1).astype(BF16)
            if rope_t is None:
                rope_t = _rope_tables(s)
            gwid = NSA_KV_HEADS * LANE
            segs = [(MIX_HEADS * LANE, BF16, True)]
            segs += [(gwid, BF16, a % 2 == 0) for a in range(6)]
            segs += [(MEM_HEADS * LANE, BF16, False), (LANE, F32, False)]
            q, kc, vc, ks_, vs_, kw, vw, qx, gl = _proj(xs, mix_norm[i], wp, segs, rope_t, tm=tm)

            n16 = s // CMP_STRIDE
            sub = CMP_STRIDE

            def to_t16(t):
                t = t.reshape(n16, sub, NSA_KV_HEADS, LANE)[..., :HEAD_DIM]
                return t.transpose(2, 0, 1, 3).reshape(NSA_KV_HEADS, n16, sub * HEAD_DIM)

            t16 = jnp.stack([to_t16(kc), to_t16(vc)])
            w1 = nsa_cmp_w1[j].astype(BF16)
            pe8 = jnp.zeros((2, SUBLANE, CMP_LEN * HEAD_DIM), F32).at[:, 0].set(
                nsa_cmp_pos[j].reshape(2, CMP_LEN * HEAD_DIM)).astype(BF16)
            w2d = jnp.pad(nsa_cmp_w2[j], ((0, 0), (0, 0), (0, LANE - HEAD_DIM))).astype(BF16)
            cmp_n, cmp_t = _cmp(t16, w1, pe8, w2d)

            n_slc = s // SLC_LEN
            n_top = min(SLC_TOPN, n_slc)
            jj = jnp.arange(n_slc)[:, None] * SLC_LEN
            nn = jnp.arange(n16)[None, :] * CMP_STRIDE
            ov = ((nn < jj + SLC_LEN) & (nn + CMP_LEN > jj) & (nn + CMP_LEN <= s)).astype(BF16)
            oc, sel = _cmpattn(q, cmp_n, cmp_t, ov, tq=tq_nsa, n_top=n_top)

            vst = vs_.reshape(s // kc_sel, kc_sel, NSA_KV_HEADS, LANE).transpose(2, 0, 3, 1)
            osel = _selattn(q, ks_, vst, sel, tq=2 * tq_nsa, ng=NSA_KV_HEADS)
            ow = _winattn(q, kw, vw, tq=tq_win)
            wmix = _pad_heads_rows(w_out[:mix_w], MIX_HEADS).astype(BF16)
            xs = _outproj(xs, [oc, osel, ow, gl], qx, mk, mv, wmix, wmem, mode="nsa", tm=tm)

        last = i == depth - 1
        xs = _ffn(xs, ffn2_norm[i], ffn2_w_in[i].astype(BF16), ffn2_w_out[i].astype(BF16),
                  final_norm if last else None, tm=tm_ffn, fc=fc)
    return xs[None]
```

```python
import functools
import math

import jax
import jax.numpy as jnp
from jax import lax
from jax.experimental import pallas as pl
from jax.experimental.pallas import tpu as pltpu

F32 = jnp.float32
BF16 = jnp.bfloat16

HEAD_DIM = 64
MIX_HEADS = 12
MEM_HEADS = 4
N_MIXERS = 3
ROT_DIM = HEAD_DIM // 4
ROPE_THETA = 500000.0
CONV_WIDTH = 3
NSA_KV_HEADS = 4
NSA_GROUP = MIX_HEADS // NSA_KV_HEADS
CMP_LEN = 32
CMP_STRIDE = 16
SLC_LEN = 64
SLC_TOPN = 16
WINDOW = 512
EPS = 1e-6

LANE = 128
SUBLANE = 8
VMEM_LIMIT_BYTES = 56 * 2**20
NEG = -1e30
SCALE = 1.0 / math.sqrt(HEAD_DIM)


def _params(*sem):
    return pltpu.CompilerParams(dimension_semantics=sem, vmem_limit_bytes=VMEM_LIMIT_BYTES)


def _const_spec(shape):
    zeros = (0,) * len(shape)
    return pl.BlockSpec(shape, lambda *_: zeros, pipeline_mode=pl.Buffered(1))


def _rms_bf16(x, g):
    ms = jnp.mean(x * x, axis=-1, keepdims=True)
    return (x * lax.rsqrt(ms + EPS) * g).astype(BF16)


def _dot(a, b):
    return jnp.dot(a, b, preferred_element_type=F32)


def _dot_nt(a, b):
    return lax.dot_general(a, b, (((1,), (1,)), ((), ())), preferred_element_type=F32)


def _ffn_body(x_ref, g_ref, wg_ref, wu_ref, wo_ref, fg_ref, o_ref, hn_ref, acc_ref, *, final_norm):
    j = pl.program_id(1)

    @pl.when(j == 0)
    def _():
        hn_ref[...] = _rms_bf16(x_ref[...], g_ref[...])

    hn = hn_ref[...]
    gate = _dot(hn, wg_ref[...])
    up = _dot(hn, wu_ref[...])
    act = (gate * jax.nn.sigmoid(gate) * up).astype(BF16)
    part = _dot(act, wo_ref[...])

    @pl.when(j == 0)
    def _():
        acc_ref[...] = part

    @pl.when(j > 0)
    def _():
        acc_ref[...] += part

    @pl.when(j == pl.num_programs(1) - 1)
    def _():
        y = x_ref[...] + 0.5 * acc_ref[...]
        if final_norm:
            ms = jnp.mean(y * y, axis=-1, keepdims=True)
            y = y * lax.rsqrt(ms + EPS) * fg_ref[...]
        o_ref[...] = y


def _ffn(x, g, w_in, w_out, final_g=None, *, tm, fc):
    s, d = x.shape
    f = w_out.shape[0]
    nf = f // fc
    fg = jnp.ones((1, d), F32) if final_g is None else final_g.reshape(1, d)
    return pl.pallas_call(
        functools.partial(_ffn_body, final_norm=final_g is not None),
        grid=(s // tm, nf),
        in_specs=[
            pl.BlockSpec((tm, d), lambda i, j: (i, 0)),
            pl.BlockSpec((1, d), lambda i, j: (0, 0)),
            pl.BlockSpec((d, fc), lambda i, j: (0, j)),
            pl.BlockSpec((d, fc), lambda i, j: (0, nf + j)),
            pl.BlockSpec((fc, d), lambda i, j: (j, 0)),
            pl.BlockSpec((1, d), lambda i, j: (0, 0)),
        ],
        out_specs=pl.BlockSpec((tm, d), lambda i, j: (i, 0)),
        out_shape=jax.ShapeDtypeStruct((s, d), F32),
        scratch_shapes=[pltpu.VMEM((tm, d), BF16), pltpu.VMEM((tm, d), F32)],
        compiler_params=_params("parallel", "arbitrary"),
        name="ffn",
    )(x, g.reshape(1, d), w_in, w_in, w_out, fg)


def _proj_body(*refs, segs, has_rope, chunk):
    n_in = 6 if has_rope else 3
    x_ref, g_ref, w_ref = refs[:3]
    outs = refs[n_in:]
    hn = _rms_bf16(x_ref[...], g_ref[...])
    if has_rope:
        cos_t, sa_t, sb_t = (r[...] for r in refs[3:6])
    col = 0
    for o_ref, (width, rope) in zip(outs, segs):
        for c0 in range(0, width, chunk):
            cw = min(chunk, width - c0)
            y = _dot(hn, w_ref[:, col + c0:col + c0 + cw])
            if rope:
                rep = cw // LANE
                cos_c, sa_c, sb_c = (jnp.concatenate([t] * rep, axis=1) if rep > 1 else t
                                     for t in (cos_t, sa_t, sb_t))
                half = ROT_DIM // 2
                y = y * cos_c + pltpu.roll(y, half, axis=1) * sa_c + pltpu.roll(y, cw - half, axis=1) * sb_c
            o_ref[:, c0:c0 + cw] = y.astype(o_ref.dtype)
        col += width


def _proj(x, g, w, segs, rope_tables=None, *, tm, chunk=256):
    s, d = x.shape
    n = w.shape[1]
    assert n == sum(sg[0] for sg in segs)
    has_rope = rope_tables is not None
    in_specs = [
        pl.BlockSpec((tm, d), lambda i: (i, 0)),
        pl.BlockSpec((1, d), lambda i: (0, 0)),
        _const_spec((d, n)),
    ]
    args = [x, g.reshape(1, d), w]
    if has_rope:
        in_specs += [pl.BlockSpec((tm, LANE), lambda i: (i, 0))] * 3
        args += list(rope_tables)
    return pl.pallas_call(
        functools.partial(_proj_body, segs=tuple((sg[0], sg[2]) for sg in segs), has_rope=has_rope,
                          chunk=chunk),
        grid=(s // tm,),
        in_specs=in_specs,
        out_specs=[pl.BlockSpec((tm, sg[0]), lambda i: (i, 0)) for sg in segs],
        out_shape=[jax.ShapeDtypeStruct((s, sg[0]), sg[1]) for sg in segs],
        compiler_params=_params("parallel"),
        name="proj",
    )(*args)


def _fgate_body(f_ref, b_ref, o_ref, carry_ref):
    i = pl.program_id(0)

    @pl.when(i == 0)
    def _():
        carry_ref[...] = jnp.zeros_like(carry_ref)

    z = f_ref[...] + b_ref[...]
    x = jnp.minimum(z, 0.0) - jnp.log(1.0 + jnp.exp(-jnp.abs(z)))
    tm = x.shape[0]
    row = lax.broadcasted_iota(jnp.int32, x.shape, 0)
    sh = 1
    while sh < tm:
        x = x + jnp.where(row >= sh, pltpu.roll(x, sh, axis=0), 0.0)
        sh *= 2
    x = x + carry_ref[...]
    carry_ref[...] = x[tm - 1:tm, :]
    o_ref[0] = x.T[:2 * SUBLANE, :]


def _fgate(f, b, *, tk):
    s = f.shape[0]
    return pl.pallas_call(
        _fgate_body,
        grid=(s // tk,),
        in_specs=[pl.BlockSpec((tk, LANE), lambda i: (i, 0)), pl.BlockSpec((1, LANE), lambda i: (0, 0))],
        out_specs=pl.BlockSpec((1, 2 * SUBLANE, tk), lambda i: (i, 0, 0)),
        out_shape=jax.ShapeDtypeStruct((s // tk, 2 * SUBLANE, tk), F32),
        scratch_shapes=[pltpu.VMEM((1, LANE), F32)],
        compiler_params=_params("arbitrary"),
        name="fgate",
    )(f, b)


UNDERFLOW = 110.0


def _fox_body(q_ref, k_ref, v_ref, ft_ref, fend_ref, o_ref, m_ref, l_ref, acc_ref, kmax2_ref, *, tq, nhs):
    hp = pl.program_id(0)
    qi = pl.program_id(1)
    m_ref[...] = jnp.full_like(m_ref, NEG)
    l_ref[...] = jnp.zeros_like(l_ref)
    acc_ref[...] = jnp.zeros_like(acc_ref)
    cols = [slice(b * LANE, (b + 1) * LANE) for b in range(nhs)]

    @pl.when(qi == 0)
    def _():
        for b in range(nhs):
            def kmax_body(c, mx, b=b):
                kf = k_ref[pl.ds(pl.multiple_of(c * tq, tq), tq), cols[b]].astype(F32)
                return jnp.maximum(mx, jnp.sum(kf * kf, axis=1, keepdims=True))

            mx = lax.fori_loop(0, k_ref.shape[0] // tq, kmax_body, jnp.zeros((tq, 1), F32))
            kmax2_ref[b] = jnp.max(mx, axis=0, keepdims=True)

    qs, f0s, kb_lo = [], [], qi
    for b in range(nhs):
        h = hp * nhs + b
        q = q_ref[:, cols[b]] * SCALE
        f0 = ft_ref[qi, pl.ds(h, 1), :][:, 0:1]
        qf = q.astype(F32)
        qmax2 = jnp.max(jnp.sum(qf * qf, axis=1, keepdims=True), axis=0, keepdims=True)
        bound = f0 + (UNDERFLOW + 2.0 * jnp.sqrt(qmax2 * kmax2_ref[b]))
        fend = fend_ref[pl.ds(h, 1), :]
        blk = lax.broadcasted_iota(jnp.int32, fend.shape, 1)
        kb_lo = jnp.minimum(kb_lo, jnp.sum(jnp.where((fend > bound) & (blk < qi), 1, 0)))
        qs.append(q)
        f0s.append(f0)

    def step(kb, causal):
        start = pl.multiple_of(kb * tq, tq)
        for b in range(nhs):
            h = hp * nhs + b
            k = k_ref[pl.ds(start, tq), cols[b]]
            v = v_ref[pl.ds(start, tq), cols[b]]
            s = _dot_nt(qs[b], k) + (f0s[b] - ft_ref[kb, pl.ds(h, 1), :])
            if causal:
                r = lax.broadcasted_iota(jnp.int32, s.shape, 0)
                c = lax.broadcasted_iota(jnp.int32, s.shape, 1)
                s = jnp.where(c <= r, s, NEG)
            m_old = m_ref[b]
            m_new = jnp.maximum(m_old, jnp.max(s, axis=1, keepdims=True))
            alpha = jnp.exp(m_old - m_new)
            p = jnp.exp(s - m_new)
            l_ref[b] = alpha * l_ref[b] + jnp.sum(p, axis=1, keepdims=True)
            acc_ref[b] = alpha * acc_ref[b] + _dot(p.astype(BF16), v)
            m_ref[b] = m_new

    def loop_body(kb, carry):
        step(kb, False)
        return carry

    lax.fori_loop(kb_lo, qi, loop_body, 0)
    step(qi, True)
    for b in range(nhs):
        o_ref[:, cols[b]] = (acc_ref[b] / l_ref[b]).astype(o_ref.dtype)


def _fox(q, k, v, ft, *, tq, nhs=1):
    s = q.shape[0]
    nh = q.shape[1] // LANE
    nq = s // tq
    assert nq <= LANE and nh % nhs == 0
    w = nhs * LANE
    fend = jnp.pad(ft[:, :, tq - 1].T, ((0, 0), (0, LANE - nq)))
    return pl.pallas_call(
        functools.partial(_fox_body, tq=tq, nhs=nhs),
        grid=(nh // nhs, nq),
        in_specs=[
            pl.BlockSpec((tq, w), lambda h, i: (i, h)),
            pl.BlockSpec((s, w), lambda h, i: (0, h)),
            pl.BlockSpec((s, w), lambda h, i: (0, h)),
            _const_spec(ft.shape),
            _const_spec(fend.shape),
        ],
        out_specs=pl.BlockSpec((tq, w), lambda h, i: (i, h)),
        out_shape=jax.ShapeDtypeStruct((s, nh * LANE), BF16),
        scratch_shapes=[pltpu.VMEM((nhs, tq, 1), F32), pltpu.VMEM((nhs, tq, 1), F32),
                        pltpu.VMEM((nhs, tq, LANE), F32), pltpu.VMEM((nhs, 1, 1), F32)],
        compiler_params=_params("arbitrary", "arbitrary"),
        name="fox",
    )(q, k, v, ft, fend)


def _conv_body(bg_ref, cg_ref, v_ref, cgh_ref, vh_ref, w_ref, o_ref):
    i = pl.program_id(0)
    u = cg_ref[...] * v_ref[...]
    halo = jnp.where(i > 0, cgh_ref[...] * vh_ref[...], 0.0)
    ext = jnp.concatenate([halo, u], axis=0)
    u1 = pltpu.roll(ext, 1, axis=0)[SUBLANE:, :]
    u2 = pltpu.roll(ext, 2, axis=0)[SUBLANE:, :]
    w = w_ref[...]
    y = w[0:1, :] * u2 + w[1:2, :] * u1 + w[2:3, :] * u
    o_ref[...] = (bg_ref[...] * y).astype(o_ref.dtype)


def _conv(bg, cg, v, w, *, tm):
    s, c = bg.shape
    w8 = jnp.zeros((SUBLANE, c), F32).at[:CONV_WIDTH].set(w)
    per = tm // SUBLANE
    main = pl.BlockSpec((tm, c), lambda i: (i, 0))
    halo = pl.BlockSpec((SUBLANE, c), lambda i: (jnp.maximum(i * per - 1, 0), 0))
    return pl.pallas_call(
        _conv_body,
        grid=(s // tm,),
        in_specs=[main, main, main, halo, halo, pl.BlockSpec((SUBLANE, c), lambda i: (0, 0))],
        out_specs=main,
        out_shape=jax.ShapeDtypeStruct((s, c), BF16),
        compiler_params=_params("parallel"),
        name="conv",
    )(bg, cg, v, cg, v, w8)


def _cmp_body(t_ref, w1_ref, pe_ref, w2_ref, on_ref, ot_ref):
    t = t_ref[0, 0]
    n16 = t.shape[0]
    half = w1_ref.shape[1] // 2
    a = _dot(t, w1_ref[0, :half, :])
    b = _dot(t, w1_ref[0, half:, :])
    pec = _dot(pe_ref[0], w1_ref[0])[0:1, :]
    pre = a + pltpu.roll(b, n16 - 1, axis=0) + pec
    hid = (pre * jax.nn.sigmoid(pre)).astype(BF16)
    out = _dot(hid, w2_ref[0])
    on_ref[0, 0] = out.astype(on_ref.dtype)
    ot_ref[0, 0] = out.T.astype(ot_ref.dtype)


def _cmp(t16, w1, pe8, w2d):
    _, g, n16, kk = t16.shape
    return pl.pallas_call(
        _cmp_body,
        grid=(2, g),
        in_specs=[
            pl.BlockSpec((1, 1, n16, kk), lambda a, b: (a, b, 0, 0)),
            pl.BlockSpec((1, 2 * kk, HEAD_DIM), lambda a, b: (a, 0, 0)),
            pl.BlockSpec((1, SUBLANE, 2 * kk), lambda a, b: (a, 0, 0)),
            pl.BlockSpec((1, HEAD_DIM, LANE), lambda a, b: (a, 0, 0)),
        ],
        out_specs=[
            pl.BlockSpec((1, 1, n16, LANE), lambda a, b: (a, b, 0, 0)),
            pl.BlockSpec((1, 1, LANE, n16), lambda a, b: (a, b, 0, 0)),
        ],
        out_shape=[
            jax.ShapeDtypeStruct((2, g, n16, LANE), BF16),
            jax.ShapeDtypeStruct((2, g, LANE, n16), BF16),
        ],
        compiler_params=_params("parallel", "parallel"),
        name="nsa_cmp",
    )(t16, w1, pe8, w2d)


def _stack_heads(q_blk):
    return jnp.concatenate([q_blk[:, j * LANE:(j + 1) * LANE] for j in range(NSA_GROUP)], axis=0) * SCALE


def _unstack_heads(o3):
    tq = o3.shape[0] // NSA_GROUP
    return jnp.concatenate([o3[j * tq:(j + 1) * tq, :] for j in range(NSA_GROUP)], axis=1)


def _cmpattn_body(q_ref, kc_ref, vct_ref, ov_ref, oc_ref, sel_ref, imp_ref, *, tq, n_top, n_bucket):
    qi = pl.program_id(1)
    nq = pl.num_programs(1)
    t0 = qi * tq
    q3 = _stack_heads(q_ref[...])
    n16 = kc_ref.shape[2]

    def attend(rows):
        st = _dot_nt(kc_ref[0, 0, :rows, :], q3)
        n_idx = lax.broadcasted_iota(jnp.int32, st.shape, 0)
        lane = lax.broadcasted_iota(jnp.int32, st.shape, 1)
        tpos = t0 + (lane & (tq - 1))
        mask = n_idx * CMP_STRIDE + (CMP_LEN - 1) <= tpos
        st = jnp.where(mask, st, NEG)
        m = jnp.max(st, axis=0, keepdims=True)
        p = jnp.where(mask, jnp.exp(st - m), 0.0)
        l = jnp.sum(p, axis=0, keepdims=True)
        inv_l = 1.0 / jnp.maximum(l, jnp.finfo(F32).tiny)
        pn = p * inv_l
        oct_ = _dot(vct_ref[0, 0, :, :rows], p.astype(BF16)) * inv_l
        oc_ref[...] = _unstack_heads(oct_.T).astype(oc_ref.dtype)
        psum = pn[:, 0:tq]
        for j in range(1, NSA_GROUP):
            psum = psum + pn[:, j * tq:(j + 1) * tq]
        p_hi = psum.astype(BF16)
        p_lo = (psum - p_hi.astype(F32)).astype(BF16)
        ov = ov_ref[:, :rows]
        imp_ref[...] = _dot(ov, p_hi) + _dot(ov, p_lo)

    for k in range(n_bucket):
        @pl.when((qi * n_bucket >= k * nq) & (qi * n_bucket < (k + 1) * nq))
        def _(k=k):
            attend((k + 1) * n16 // n_bucket)

    imp = imp_ref[...]
    jf = lax.broadcasted_iota(jnp.int32, imp.shape, 0).astype(F32)
    tl = t0 + lax.broadcasted_iota(jnp.int32, imp.shape, 1)
    tblk = jnp.right_shift(tl, SLC_LEN.bit_length() - 1).astype(F32)
    valid = jf <= tblk
    forced = (jf == 0.0) | (jf == tblk) | (jf == tblk - 1.0)
    sel = jnp.where(valid & forced, 1.0, 0.0)
    score = jnp.where(valid & jnp.logical_not(forced), imp, -jnp.inf)
    big = float(imp.shape[0])
    for _ in range(max(n_top - 3, 0)):
        mx = jnp.max(score, axis=0, keepdims=True)
        idx = jnp.min(jnp.where(score == mx, jf, big), axis=0, keepdims=True)
        hit = jf == idx
        sel = jnp.where(hit & (mx > -jnp.inf), 1.0, sel)
        score = jnp.where(hit, -jnp.inf, score)
    sel_ref[0, 0] = sel


def _cmpattn(q, kcmp, vcmpt, ov, *, tq, n_top):
    s = q.shape[0]
    g, n16 = kcmp.shape[1], kcmp.shape[2]
    n_slc = ov.shape[0]
    nq = s // tq
    gw = NSA_GROUP * LANE
    n_bucket = 4 if n16 % (4 * LANE) == 0 and nq % 4 == 0 else 1
    return pl.pallas_call(
        functools.partial(_cmpattn_body, tq=tq, n_top=n_top, n_bucket=n_bucket),
        grid=(g, nq),
        in_specs=[
            pl.BlockSpec((tq, gw), lambda a, i: (i, a)),
            pl.BlockSpec((1, 1, n16, LANE), lambda a, i: (0, a, 0, 0)),
            pl.BlockSpec((1, 1, LANE, n16), lambda a, i: (1, a, 0, 0)),
            _const_spec(ov.shape),
        ],
        out_specs=[
            pl.BlockSpec((tq, gw), lambda a, i: (i, a)),
            pl.BlockSpec((1, 1, n_slc, tq), lambda a, i: (a, i, 0, 0)),
        ],
        out_shape=[
            jax.ShapeDtypeStruct((s, g * gw), F32),
            jax.ShapeDtypeStruct((g, nq, n_slc, tq), F32),
        ],
        scratch_shapes=[pltpu.VMEM((n_slc, tq), F32)],
        compiler_params=_params("parallel", "arbitrary"),
        name="nsa_cmpattn",
    )(q, kcmp, vcmpt, ov)


def _selattn_body(q_ref, k_ref, vt_ref, sel_ref, o_ref, m_ref, l_ref, acc_ref, *, tq, kc, ng):
    qi = pl.program_id(1)
    gw = NSA_GROUP * LANE
    q3s = [_stack_heads(q_ref[:, b * gw:(b + 1) * gw]) for b in range(ng)]
    m_ref[...] = jnp.full_like(m_ref, NEG)
    l_ref[...] = jnp.zeros_like(l_ref)
    acc_ref[...] = jnp.zeros_like(acc_ref)
    bpc = kc // SLC_LEN
    n_full = (qi * tq) // kc

    def chunk(c, causal):
        start = pl.multiple_of(c * kc, kc)
        for b in range(ng):
            k = k_ref[pl.ds(start, kc), b * LANE:(b + 1) * LANE]
            st = _dot_nt(k, q3s[b])
            blk_rows = pl.ds(pl.multiple_of(c * bpc, bpc), bpc)
            srows = [sel_ref[b, u, blk_rows, :] for u in range(tq // sel_ref.shape[3])]
            pen = (jnp.concatenate(srows * NSA_GROUP, axis=1) - 1.0) * (-NEG)
            st = st + jnp.concatenate(
                [jnp.broadcast_to(pen[r:r + 1, :], (SLC_LEN, pen.shape[1])) for r in range(bpc)], axis=0)
            if causal:
                kpos = c * kc + lax.broadcasted_iota(jnp.int32, st.shape, 0)
                tpos = qi * tq + (lax.broadcasted_iota(jnp.int32, st.shape, 1) & (tq - 1))
                st = jnp.where(kpos <= tpos, st, NEG)
            m_old = m_ref[b]
            m_new = jnp.maximum(m_old, jnp.max(st, axis=0, keepdims=True))
            alpha = jnp.exp(m_old - m_new)
            p = jnp.exp(st - m_new)
            l_ref[b] = alpha * l_ref[b] + jnp.sum(p, axis=0, keepdims=True)
            acc_ref[b] = alpha * acc_ref[b] + _dot(vt_ref[b, c], p.astype(BF16))
            m_ref[b] = m_new

    def loop_body(c, carry):
        chunk(c, False)
        return carry

    lax.fori_loop(0, n_full, loop_body, 0)
    chunk(n_full, True)
    for b in range(ng):
        o_ref[:, b * gw:(b + 1) * gw] = _unstack_heads((acc_ref[b] / l_ref[b]).T).astype(o_ref.dtype)


def _selattn(q, ks, vst, sel, *, tq, ng):
    s = q.shape[0]
    g, _, n_slc, tsel = sel.shape
    _, nchunk, _, kc = vst.shape
    gw = NSA_GROUP * LANE
    sw = NSA_GROUP * tq
    resident = dict(pipeline_mode=pl.Buffered(1)) if g == ng else {}
    return pl.pallas_call(
        functools.partial(_selattn_body, tq=tq, kc=kc, ng=ng),
        grid=(g // ng, s // tq),
        in_specs=[
            pl.BlockSpec((tq, ng * gw), lambda a, i: (i, a)),
            pl.BlockSpec((s, ng * LANE), lambda a, i: (0, a), **resident),
            pl.BlockSpec((ng, nchunk, LANE, kc), lambda a, i: (a, 0, 0, 0), **resident),
            pl.BlockSpec((ng, tq // tsel, n_slc, tsel), lambda a, i: (a, i, 0, 0)),
        ],
        out_specs=pl.BlockSpec((tq, ng * gw), lambda a, i: (i, a)),
        out_shape=jax.ShapeDtypeStruct((s, g * gw), F32),
        scratch_shapes=[pltpu.VMEM((ng, 1, sw), F32), pltpu.VMEM((ng, 1, sw), F32),
                        pltpu.VMEM((ng, LANE, sw), F32)],
        compiler_params=_params("parallel", "arbitrary"),
        name="nsa_selattn",
    )(q, ks, vst, sel)


def _winattn_body(*refs, tq, nwb):
    q_ref, band_ref = refs[0], refs[1]
    k_refs = refs[2:3 + nwb]
    v_refs = refs[3 + nwb:4 + 2 * nwb]
    o_ref = refs[4 + 2 * nwb]
    qi = pl.program_id(1)
    q3 = _stack_heads(q_ref[...])
    k = jnp.concatenate([r[...] for r in k_refs], axis=0)
    v = jnp.concatenate([r[...] for r in v_refs], axis=0)
    s = _dot_nt(q3, k)
    col = lax.broadcasted_iota(jnp.int32, (1, s.shape[1]), 1)
    before_start = jnp.where(col < (nwb - qi) * tq, NEG, 0.0)
    s = s + jnp.concatenate([band_ref[...]] * NSA_GROUP, axis=0) + before_start
    m = jnp.max(s, axis=1, keepdims=True)
    p = jnp.exp(s - m)
    l = jnp.sum(p, axis=1, keepdims=True)
    o3 = _dot(p.astype(BF16), v) / l
    o_ref[...] = _unstack_heads(o3).astype(o_ref.dtype)


def _winattn(q, kw, vw, *, tq):
    s = q.shape[0]
    g = kw.shape[1] // LANE
    nwb = WINDOW // tq
    gw = NSA_GROUP * LANE

    def kv_spec(back):
        return pl.BlockSpec((tq, LANE), lambda a, i: (jnp.maximum(i - back, 0), a))

    kv_specs = [kv_spec(nwb - b) for b in range(nwb + 1)]
    dist = jnp.arange(tq)[:, None] - (jnp.arange((nwb + 1) * tq)[None, :] - nwb * tq)
    band = jnp.where((dist >= 0) & (dist < WINDOW), 0.0, NEG).astype(F32)
    return pl.pallas_call(
        functools.partial(_winattn_body, tq=tq, nwb=nwb),
        grid=(g, s // tq),
        in_specs=[pl.BlockSpec((tq, gw), lambda a, i: (i, a)), _const_spec(band.shape)] + kv_specs + kv_specs,
        out_specs=pl.BlockSpec((tq, gw), lambda a, i: (i, a)),
        out_shape=jax.ShapeDtypeStruct((s, g * gw), F32),
        compiler_params=_params("parallel", "parallel"),
        name="nsa_winattn",
    )(q, band, *([kw] * (nwb + 1)), *([vw] * (nwb + 1)))


def _outproj_body(*refs, mode):
    if mode == "nsa":
        x_ref, oc_ref, os_ref, ow_ref, gl_ref, qx_ref, mk_ref, mv_ref, wmix_ref, wmem_ref, o_ref = refs
        gate = jax.nn.sigmoid(gl_ref[...])
        parts = []
        for hd in range(MIX_HEADS):
            sl = slice(hd * LANE, (hd + 1) * LANE)
            acc = None
            for b, br in enumerate((oc_ref, os_ref, ow_ref)):
                c = b * MIX_HEADS + hd
                term = gate[:, c:c + 1] * br[:, sl]
                acc = term if acc is None else acc + term
            parts.append(acc.astype(BF16))
        ymix = jnp.concatenate(parts, axis=1)
    else:
        x_ref, ymix_ref, qx_ref, mk_ref, mv_ref, wmix_ref, wmem_ref, o_ref = refs
        ymix = ymix_ref[...]
    y = _dot(ymix, wmix_ref[...])
    for hd in range(MEM_HEADS):
        sl = slice(hd * LANE, (hd + 1) * LANE)
        s = _dot_nt(qx_ref[:, sl], mk_ref[:, sl]) * SCALE
        m = jnp.max(s, axis=1, keepdims=True)
        p = jnp.exp(s - m)
        p = p / jnp.sum(p, axis=1, keepdims=True)
        ymem = _dot(p.astype(BF16), mv_ref[:, sl]).astype(BF16)
        y = y + _dot(ymem, wmem_ref[sl, :])
    o_ref[...] = x_ref[...] + y


def _outproj(x, mix_args, qx, mk, mv, wmix, wmem, *, mode, tm):
    s, d = x.shape
    row = lambda w: pl.BlockSpec((tm, w), lambda i: (i, 0))
    mix_specs = [row(a.shape[1]) for a in mix_args]
    return pl.pallas_call(
        functools.partial(_outproj_body, mode=mode),
        grid=(s // tm,),
        in_specs=[row(d)] + mix_specs + [row(qx.shape[1]), _const_spec(mk.shape), _const_spec(mv.shape),
                                         _const_spec(wmix.shape), _const_spec(wmem.shape)],
        out_specs=row(d),
        out_shape=jax.ShapeDtypeStruct((s, d), F32),
        compiler_params=_params("parallel"),
        name="outproj",
    )(x, *mix_args, qx, mk, mv, wmix, wmem)


def _pad_heads_cols(w, n_heads):
    d = w.shape[0]
    w = w.reshape(d, n_heads, HEAD_DIM)
    return jnp.pad(w, ((0, 0), (0, 0), (0, LANE - HEAD_DIM))).reshape(d, n_heads * LANE)


def _pad_heads_rows(w, n_heads):
    d = w.shape[1]
    w = w.reshape(n_heads, HEAD_DIM, d)
    return jnp.pad(w, ((0, 0), (0, LANE - HEAD_DIM), (0, 0))).reshape(n_heads * LANE, d)


def _pad_cols(w, width):
    return jnp.pad(w, ((0, 0), (0, width - w.shape[1])))


def _rope_tables(s):
    half = ROT_DIM // 2
    inv = ROPE_THETA ** (-jnp.arange(half, dtype=F32) / half)
    ang = jnp.arange(s, dtype=F32)[:, None] * inv[None, :]
    cos, sin = jnp.cos(ang), jnp.sin(ang)
    z = lambda n: jnp.zeros((s, n), F32)
    cos_t = jnp.concatenate([cos, cos, jnp.ones((s, LANE - ROT_DIM), F32)], axis=1)
    sa_t = jnp.concatenate([z(half), sin, z(LANE - ROT_DIM)], axis=1)
    sb_t = jnp.concatenate([-sin, z(LANE - half)], axis=1)
    return cos_t, sa_t, sb_t


def _pick(n, candidates):
    for c in candidates:
        if n % c == 0:
            return c
    raise ValueError(f"no tile size for {n}")


def kernel(x, mem, ffn1_norm, ffn1_w_in, ffn1_w_out, mix_norm, mix_w_out, mem_norm, mem_w_kv, fox_w_in,
           fox_b_f, conv_w_in, conv_w, nsa_w_in, nsa_cmp_pos, nsa_cmp_w1, nsa_cmp_w2, ffn2_norm, ffn2_w_in,
           ffn2_w_out, final_norm):
    b, s, d = x.shape
    assert b == 1
    depth = ffn1_norm.shape[0]
    mix_w = MIX_HEADS * HEAD_DIM
    mem_w = MEM_HEADS * HEAD_DIM
    kv_w = NSA_KV_HEADS * HEAD_DIM
    d_ff = ffn1_w_out.shape[1]
    tm_ffn = _pick(s, (1024, 512, 256, 128))
    fc = next(c for c in (d_ff // 4, d_ff // 2, 256, LANE) if d_ff % c == 0 and c % LANE == 0)
    tm = _pick(s, (512, 256, 128))
    tq_fox = _pick(s, (512, 256, 128))
    tq_nsa = 128
    kc_sel = _pick(s, (512, 256, 128))
    tq_win = _pick(s, (256, 128))
    m_len = mem.shape[1]

    xs = x[0]
    mem_s = mem[0]
    rope_t = None
    for i in range(depth):
        xs = _ffn(xs, ffn1_norm[i], ffn1_w_in[i].astype(BF16), ffn1_w_out[i].astype(BF16), tm=tm_ffn, fc=fc)

        wkv = mem_w_kv[i]
        wkv_p = jnp.concatenate([_pad_heads_cols(wkv[:, :mem_w], MEM_HEADS),
                                 _pad_heads_cols(wkv[:, mem_w:], MEM_HEADS)], axis=1).astype(BF16)
        mk, mv = _proj(mem_s, mem_norm, wkv_p, [(MEM_HEADS * LANE, BF16, False)] * 2, tm=m_len)

        w_out = mix_w_out[i]
        wmem = _pad_heads_rows(w_out[mix_w:], MEM_HEADS).astype(BF16)
        kind, j = i % N_MIXERS, i // N_MIXERS
        if kind == 0:
            w = fox_w_in[j]
            wq, wk, wv = (w[:, a * mix_w:(a + 1) * mix_w] for a in range(3))
            wf = w[:, 3 * mix_w:3 * mix_w + MIX_HEADS]
            wqx = w[:, 3 * mix_w + MIX_HEADS:]
            wp = jnp.concatenate([_pad_heads_cols(wq, MIX_HEADS), _pad_heads_cols(wk, MIX_HEADS),
                                  _pad_heads_cols(wv, MIX_HEADS), _pad_heads_cols(wqx, MEM_HEADS),
                                  _pad_cols(wf, LANE)], axis=1).astype(BF16)
            hw = MIX_HEADS * LANE
            q, k, v, qx, f = _proj(xs, mix_norm[i], wp,
                                   [(hw, BF16, False)] * 3 + [(MEM_HEADS * LANE, BF16, False), (LANE, F32, False)],
                                   tm=tm)
            ft = _fgate(f, _pad_cols(fox_b_f[j].reshape(1, MIX_HEADS), LANE), tk=tq_fox)
            ymix = _fox(q, k, v, ft, tq=tq_fox)
            wmix = _pad_heads_rows(w_out[:mix_w], MIX_HEADS).astype(BF16)
            xs = _outproj(xs, [ymix], qx, mk, mv, wmix, wmem, mode="plain", tm=tm)
        elif kind == 1:
            w = conv_w_in[j]
            wp = jnp.concatenate([w[:, :3 * mix_w], _pad_heads_cols(w[:, 3 * mix_w:], MEM_HEADS)],
                                 axis=1).astype(BF16)
            bg, cg, v, qx = _proj(xs, mix_norm[i], wp,
                                  [(mix_w, F32, False)] * 3 + [(MEM_HEADS * LANE, BF16, False)], tm=tm)
            ymix = _conv(bg, cg, v, conv_w[j], tm=tm)
            xs = _outproj(xs, [ymix], qx, mk, mv, w_out[:mix_w].astype(BF16), wmem, mode="plain", tm=tm)
        else:
            w = nsa_w_in[j]
            off = mix_w
            wq = w[:, :mix_w]
            kvs = []
            for _ in range(6):
                kvs.append(_pad_heads_cols(w[:, off:off + kv_w], NSA_KV_HEADS))
                off += kv_w
            wgl = w[:, off:off + 3 * MIX_HEADS]
            wqx = w[:, off + 3 * MIX_HEADS:]
            wp = jnp.concatenate([_pad_heads_cols(wq, MIX_HEADS)] + kvs +
                                 [_pad_heads_cols(wqx, MEM_HEADS), _pad_cols(wgl, LANE)], axis=1).astype(BF16)
            if rope_t is None:
                rope_t = _rope_tables(s)
            gwid = NSA_KV_HEADS * LANE
            segs = [(MIX_HEADS * LANE, BF16, True)]
            segs += [(gwid, BF16, a % 2 == 0) for a in range(6)]
            segs += [(MEM_HEADS * LANE, BF16, False), (LANE, F32, False)]
            q, kc, vc, ks_, vs_, kw, vw, qx, gl = _proj(xs, mix_norm[i], wp, segs, rope_t, tm=tm)

            n16 = s // CMP_STRIDE
            sub = CMP_STRIDE

            def to_t16(t):
                t = t.reshape(n16, sub, NSA_KV_HEADS, LANE)[..., :HEAD_DIM]
                return t.transpose(2, 0, 1, 3).reshape(NSA_KV_HEADS, n16, sub * HEAD_DIM)

            t16 = jnp.stack([to_t16(kc), to_t16(vc)])
            w1 = nsa_cmp_w1[j].astype(BF16)
            pe8 = jnp.zeros((2, SUBLANE, CMP_LEN * HEAD_DIM), F32).at[:, 0].set(
                nsa_cmp_pos[j].reshape(2, CMP_LEN * HEAD_DIM)).astype(BF16)
            w2d = jnp.pad(nsa_cmp_w2[j], ((0, 0), (0, 0), (0, LANE - HEAD_DIM))).astype(BF16)
            cmp_n, cmp_t = _cmp(t16, w1, pe8, w2d)

            n_slc = s // SLC_LEN
            n_top = min(SLC_TOPN, n_slc)
            jj = jnp.arange(n_slc)[:, None] * SLC_LEN
            nn = jnp.arange(n16)[None, :] * CMP_STRIDE
            ov = ((nn < jj + SLC_LEN) & (nn + CMP_LEN > jj) & (nn + CMP_LEN <= s)).astype(BF16)
            oc, sel = _cmpattn(q, cmp_n, cmp_t, ov, tq=tq_nsa, n_top=n_top)

            vst = vs_.reshape(s // kc_sel, kc_sel, NSA_KV_HEADS, LANE).transpose(2, 0, 3, 1)
            osel = _selattn(q, ks_, vst, sel, tq=2 * tq_nsa, ng=NSA_KV_HEADS)
            ow = _winattn(q, kw, vw, tq=tq_win)
            wmix = _pad_heads_rows(w_out[:mix_w], MIX_HEADS).astype(BF16)
            xs = _outproj(xs, [oc, osel, ow, gl], qx, mk, mv, wmix, wmem, mode="nsa", tm=tm)

        last = i == depth - 1
        xs = _ffn(xs, ffn2_norm[i], ffn2_w_in[i].astype(BF16), ffn2_w_out[i].astype(BF16),
                  final_norm if last else None, tm=tm_ffn, fc=fc)
    return xs[None]
```

```python
import functools
import math

import jax
import jax.numpy as jnp
from jax import lax
from jax.experimental import pallas as pl
from jax.experimental.pallas import tpu as pltpu

F32 = jnp.float32
BF16 = jnp.bfloat16

HEAD_DIM = 64
MIX_HEADS = 12
MEM_HEADS = 4
N_MIXERS = 3
ROT_DIM = HEAD_DIM // 4
ROPE_THETA = 500000.0
CONV_WIDTH = 3
NSA_KV_HEADS = 4
NSA_GROUP = MIX_HEADS // NSA_KV_HEADS
CMP_LEN = 32
CMP_STRIDE = 16
SLC_LEN = 64
SLC_TOPN = 16
WINDOW = 512
EPS = 1e-6

LANE = 128
SUBLANE = 8
VMEM_LIMIT_BYTES = 56 * 2**20
NEG = -1e30
SCALE = 1.0 / math.sqrt(HEAD_DIM)


def _params(*sem):
    return pltpu.CompilerParams(dimension_semantics=sem, vmem_limit_bytes=VMEM_LIMIT_BYTES)


def _const_spec(shape):
    zeros = (0,) * len(shape)
    return pl.BlockSpec(shape, lambda *_: zeros, pipeline_mode=pl.Buffered(1))


def _rms_bf16(x, g):
    ms = jnp.mean(x * x, axis=-1, keepdims=True)
    return (x * lax.rsqrt(ms + EPS) * g).astype(BF16)


def _dot(a, b):
    return jnp.dot(a, b, preferred_element_type=F32)


def _dot_nt(a, b):
    return lax.dot_general(a, b, (((1,), (1,)), ((), ())), preferred_element_type=F32)


def _ffn_body(x_ref, g_ref, wg_ref, wu_ref, wo_ref, fg_ref, o_ref, hn_ref, acc_ref, *, final_norm):
    j = pl.program_id(1)

    @pl.when(j == 0)
    def _():
        hn_ref[...] = _rms_bf16(x_ref[...], g_ref[...])

    hn = hn_ref[...]
    gate = _dot(hn, wg_ref[...])
    up = _dot(hn, wu_ref[...])
    act = (gate * jax.nn.sigmoid(gate) * up).astype(BF16)
    part = _dot(act, wo_ref[...])

    @pl.when(j == 0)
    def _():
        acc_ref[...] = part

    @pl.when(j > 0)
    def _():
        acc_ref[...] += part

    @pl.when(j == pl.num_programs(1) - 1)
    def _():
        y = x_ref[...] + 0.5 * acc_ref[...]
        if final_norm:
            ms = jnp.mean(y * y, axis=-1, keepdims=True)
            y = y * lax.rsqrt(ms + EPS) * fg_ref[...]
        o_ref[...] = y


def _ffn(x, g, w_in, w_out, final_g=None, *, tm, fc):
    s, d = x.shape
    f = w_out.shape[0]
    nf = f // fc
    fg = jnp.ones((1, d), F32) if final_g is None else final_g.reshape(1, d)
    resident = dict(pipeline_mode=pl.Buffered(1)) if nf == 1 else {}
    return pl.pallas_call(
        functools.partial(_ffn_body, final_norm=final_g is not None),
        grid=(s // tm, nf),
        in_specs=[
            pl.BlockSpec((tm, d), lambda i, j: (i, 0)),
            pl.BlockSpec((1, d), lambda i, j: (0, 0)),
            pl.BlockSpec((d, fc), lambda i, j: (0, j), **resident),
            pl.BlockSpec((d, fc), lambda i, j: (0, nf + j), **resident),
            pl.BlockSpec((fc, d), lambda i, j: (j, 0), **resident),
            pl.BlockSpec((1, d), lambda i, j: (0, 0)),
        ],
        out_specs=pl.BlockSpec((tm, d), lambda i, j: (i, 0)),
        out_shape=jax.ShapeDtypeStruct((s, d), F32),
        scratch_shapes=[pltpu.VMEM((tm, d), BF16), pltpu.VMEM((tm, d), F32)],
        compiler_params=_params("parallel", "arbitrary"),
        name="ffn",
    )(x, g.reshape(1, d), w_in, w_in, w_out, fg)


def _proj_body(*refs, segs, has_rope, chunk):
    n_in = 6 if has_rope else 3
    x_ref, g_ref, w_ref = refs[:3]
    outs = refs[n_in:]
    hn = _rms_bf16(x_ref[...], g_ref[...])
    if has_rope:
        cos_t, sa_t, sb_t = (r[...] for r in refs[3:6])
    col = 0
    for o_ref, (width, rope) in zip(outs, segs):
        for c0 in range(0, width, chunk):
            cw = min(chunk, width - c0)
            y = _dot(hn, w_ref[:, col + c0:col + c0 + cw])
            if rope:
                rep = cw // LANE
                cos_c, sa_c, sb_c = (jnp.concatenate([t] * rep, axis=1) if rep > 1 else t
                                     for t in (cos_t, sa_t, sb_t))
                half = ROT_DIM // 2
                y = y * cos_c + pltpu.roll(y, half, axis=1) * sa_c + pltpu.roll(y, cw - half, axis=1) * sb_c
            o_ref[:, c0:c0 + cw] = y.astype(o_ref.dtype)
        col += width


def _proj(x, g, w, segs, rope_tables=None, *, tm, chunk=256):
    s, d = x.shape
    n = w.shape[1]
    assert n == sum(sg[0] for sg in segs)
    has_rope = rope_tables is not None
    in_specs = [
        pl.BlockSpec((tm, d), lambda i: (i, 0)),
        pl.BlockSpec((1, d), lambda i: (0, 0)),
        _const_spec((d, n)),
    ]
    args = [x, g.reshape(1, d), w]
    if has_rope:
        in_specs += [pl.BlockSpec((tm, LANE), lambda i: (i, 0))] * 3
        args += list(rope_tables)
    return pl.pallas_call(
        functools.partial(_proj_body, segs=tuple((sg[0], sg[2]) for sg in segs), has_rope=has_rope,
                          chunk=chunk),
        grid=(s // tm,),
        in_specs=in_specs,
        out_specs=[pl.BlockSpec((tm, sg[0]), lambda i: (i, 0)) for sg in segs],
        out_shape=[jax.ShapeDtypeStruct((s, sg[0]), sg[1]) for sg in segs],
        compiler_params=_params("parallel"),
        name="proj",
    )(*args)


def _fgate_body(f_ref, b_ref, o_ref, carry_ref):
    i = pl.program_id(0)

    @pl.when(i == 0)
    def _():
        carry_ref[...] = jnp.zeros_like(carry_ref)

    z = f_ref[...] + b_ref[...]
    x = jnp.minimum(z, 0.0) - jnp.log(1.0 + jnp.exp(-jnp.abs(z)))
    tm = x.shape[0]
    row = lax.broadcasted_iota(jnp.int32, x.shape, 0)
    sh = 1
    while sh < tm:
        x = x + jnp.where(row >= sh, pltpu.roll(x, sh, axis=0), 0.0)
        sh *= 2
    x = x + carry_ref[...]
    carry_ref[...] = x[tm - 1:tm, :]
    o_ref[0] = x.T[:2 * SUBLANE, :]


def _fgate(f, b, *, tk):
    s = f.shape[0]
    return pl.pallas_call(
        _fgate_body,
        grid=(s // tk,),
        in_specs=[pl.BlockSpec((tk, LANE), lambda i: (i, 0)), pl.BlockSpec((1, LANE), lambda i: (0, 0))],
        out_specs=pl.BlockSpec((1, 2 * SUBLANE, tk), lambda i: (i, 0, 0)),
        out_shape=jax.ShapeDtypeStruct((s // tk, 2 * SUBLANE, tk), F32),
        scratch_shapes=[pltpu.VMEM((1, LANE), F32)],
        compiler_params=_params("arbitrary"),
        name="fgate",
    )(f, b)


UNDERFLOW = 110.0


def _fox_body(q_ref, k_ref, v_ref, ft_ref, fend_ref, o_ref, m_ref, l_ref, acc_ref, kmax2_ref, *, tq, nhs):
    hp = pl.program_id(0)
    qi = pl.program_id(1)
    m_ref[...] = jnp.full_like(m_ref, NEG)
    l_ref[...] = jnp.zeros_like(l_ref)
    acc_ref[...] = jnp.zeros_like(acc_ref)
    cols = [slice(b * LANE, (b + 1) * LANE) for b in range(nhs)]

    @pl.when(qi == 0)
    def _():
        for b in range(nhs):
            def kmax_body(c, mx, b=b):
                kf = k_ref[pl.ds(pl.multiple_of(c * tq, tq), tq), cols[b]].astype(F32)
                return jnp.maximum(mx, jnp.sum(kf * kf, axis=1, keepdims=True))

            mx = lax.fori_loop(0, k_ref.shape[0] // tq, kmax_body, jnp.zeros((tq, 1), F32))
            kmax2_ref[b] = jnp.max(mx, axis=0, keepdims=True)

    qs, f0s, kb_lo = [], [], qi
    for b in range(nhs):
        h = hp * nhs + b
        q = q_ref[:, cols[b]] * SCALE
        f0 = ft_ref[qi, pl.ds(h, 1), :][:, 0:1]
        qf = q.astype(F32)
        qmax2 = jnp.max(jnp.sum(qf * qf, axis=1, keepdims=True), axis=0, keepdims=True)
        bound = f0 + (UNDERFLOW + 2.0 * jnp.sqrt(qmax2 * kmax2_ref[b]))
        fend = fend_ref[pl.ds(h, 1), :]
        blk = lax.broadcasted_iota(jnp.int32, fend.shape, 1)
        kb_lo = jnp.minimum(kb_lo, jnp.sum(jnp.where((fend > bound) & (blk < qi), 1, 0)))
        qs.append(q)
        f0s.append(f0)

    def step(kb, causal):
        start = pl.multiple_of(kb * tq, tq)
        for b in range(nhs):
            h = hp * nhs + b
            k = k_ref[pl.ds(start, tq), cols[b]]
            v = v_ref[pl.ds(start, tq), cols[b]]
            s = _dot_nt(qs[b], k) + (f0s[b] - ft_ref[kb, pl.ds(h, 1), :])
            if causal:
                r = lax.broadcasted_iota(jnp.int32, s.shape, 0)
                c = lax.broadcasted_iota(jnp.int32, s.shape, 1)
                s = jnp.where(c <= r, s, NEG)
            m_old = m_ref[b]
            m_new = jnp.maximum(m_old, jnp.max(s, axis=1, keepdims=True))
            alpha = jnp.exp(m_old - m_new)
            p = jnp.exp(s - m_new)
            l_ref[b] = alpha * l_ref[b] + jnp.sum(p, axis=1, keepdims=True)
            acc_ref[b] = alpha * acc_ref[b] + _dot(p.astype(BF16), v)
            m_ref[b] = m_new

    def loop_body(kb, carry):
        step(kb, False)
        return carry

    lax.fori_loop(kb_lo, qi, loop_body, 0)
    step(qi, True)
    for b in range(nhs):
        o_ref[:, cols[b]] = (acc_ref[b] / l_ref[b]).astype(o_ref.dtype)


def _fox(q, k, v, ft, *, tq, nhs=1):
    s = q.shape[0]
    nh = q.shape[1] // LANE
    nq = s // tq
    assert nq <= LANE and nh % nhs == 0
    w = nhs * LANE
    fend = jnp.pad(ft[:, :, tq - 1].T, ((0, 0), (0, LANE - nq)))
    return pl.pallas_call(
        functools.partial(_fox_body, tq=tq, nhs=nhs),
        grid=(nh // nhs, nq),
        in_specs=[
            pl.BlockSpec((tq, w), lambda h, i: (i, h)),
            pl.BlockSpec((s, w), lambda h, i: (0, h)),
            pl.BlockSpec((s, w), lambda h, i: (0, h)),
            _const_spec(ft.shape),
            _const_spec(fend.shape),
        ],
        out_specs=pl.BlockSpec((tq, w), lambda h, i: (i, h)),
        out_shape=jax.ShapeDtypeStruct((s, nh * LANE), BF16),
        scratch_shapes=[pltpu.VMEM((nhs, tq, 1), F32), pltpu.VMEM((nhs, tq, 1), F32),
                        pltpu.VMEM((nhs, tq, LANE), F32), pltpu.VMEM((nhs, 1, 1), F32)],
        compiler_params=_params("arbitrary", "arbitrary"),
        name="fox",
    )(q, k, v, ft, fend)


def _conv_body(bg_ref, cg_ref, v_ref, cgh_ref, vh_ref, w_ref, o_ref):
    i = pl.program_id(0)
    u = cg_ref[...] * v_ref[...]
    halo = jnp.where(i > 0, cgh_ref[...] * vh_ref[...], 0.0)
    ext = jnp.concatenate([halo, u], axis=0)
    u1 = pltpu.roll(ext, 1, axis=0)[SUBLANE:, :]
    u2 = pltpu.roll(ext, 2, axis=0)[SUBLANE:, :]
    w = w_ref[...]
    y = w[0:1, :] * u2 + w[1:2, :] * u1 + w[2:3, :] * u
    o_ref[...] = (bg_ref[...] * y).astype(o_ref.dtype)


def _conv(bg, cg, v, w, *, tm):
    s, c = bg.shape
    w8 = jnp.zeros((SUBLANE, c), F32).at[:CONV_WIDTH].set(w)
    per = tm // SUBLANE
    main = pl.BlockSpec((tm, c), lambda i: (i, 0))
    halo = pl.BlockSpec((SUBLANE, c), lambda i: (jnp.maximum(i * per - 1, 0), 0))
    return pl.pallas_call(
        _conv_body,
        grid=(s // tm,),
        in_specs=[main, main, main, halo, halo, pl.BlockSpec((SUBLANE, c), lambda i: (0, 0))],
        out_specs=main,
        out_shape=jax.ShapeDtypeStruct((s, c), BF16),
        compiler_params=_params("parallel"),
        name="conv",
    )(bg, cg, v, cg, v, w8)


def _cmp_body(t_ref, w1_ref, pe_ref, w2_ref, on_ref, ot_ref):
    t = t_ref[0, 0]
    n16 = t.shape[0]
    half = w1_ref.shape[1] // 2
    a = _dot(t, w1_ref[0, :half, :])
    b = _dot(t, w1_ref[0, half:, :])
    pec = _dot(pe_ref[0], w1_ref[0])[0:1, :]
    pre = a + pltpu.roll(b, n16 - 1, axis=0) + pec
    hid = (pre * jax.nn.sigmoid(pre)).astype(BF16)
    out = _dot(hid, w2_ref[0])
    on_ref[0, 0] = out.astype(on_ref.dtype)
    ot_ref[0, 0] = out.T.astype(ot_ref.dtype)


def _cmp(t16, w1, pe8, w2d):
    _, g, n16, kk = t16.shape
    return pl.pallas_call(
        _cmp_body,
        grid=(2, g),
        in_specs=[
            pl.BlockSpec((1, 1, n16, kk), lambda a, b: (a, b, 0, 0)),
            pl.BlockSpec((1, 2 * kk, HEAD_DIM), lambda a, b: (a, 0, 0)),
            pl.BlockSpec((1, SUBLANE, 2 * kk), lambda a, b: (a, 0, 0)),
            pl.BlockSpec((1, HEAD_DIM, LANE), lambda a, b: (a, 0, 0)),
        ],
        out_specs=[
            pl.BlockSpec((1, 1, n16, LANE), lambda a, b: (a, b, 0, 0)),
            pl.BlockSpec((1, 1, LANE, n16), lambda a, b: (a, b, 0, 0)),
        ],
        out_shape=[
            jax.ShapeDtypeStruct((2, g, n16, LANE), BF16),
            jax.ShapeDtypeStruct((2, g, LANE, n16), BF16),
        ],
        compiler_params=_params("parallel", "parallel"),
        name="nsa_cmp",
    )(t16, w1, pe8, w2d)


def _stack_heads(q_blk):
    return jnp.concatenate([q_blk[:, j * LANE:(j + 1) * LANE] for j in range(NSA_GROUP)], axis=0) * SCALE


def _unstack_heads(o3):
    tq = o3.shape[0] // NSA_GROUP
    return jnp.concatenate([o3[j * tq:(j + 1) * tq, :] for j in range(NSA_GROUP)], axis=1)


def _cmpattn_body(q_ref, kc_ref, vct_ref, ov_ref, oc_ref, sel_ref, imp_ref, *, tq, n_top, n_bucket):
    qi = pl.program_id(1)
    nq = pl.num_programs(1)
    t0 = qi * tq
    q3 = _stack_heads(q_ref[...])
    n16 = kc_ref.shape[2]

    def attend(rows):
        st = _dot_nt(kc_ref[0, 0, :rows, :], q3)
        n_idx = lax.broadcasted_iota(jnp.int32, st.shape, 0)
        lane = lax.broadcasted_iota(jnp.int32, st.shape, 1)
        tpos = t0 + (lane & (tq - 1))
        mask = n_idx * CMP_STRIDE + (CMP_LEN - 1) <= tpos
        st = jnp.where(mask, st, NEG)
        m = jnp.max(st, axis=0, keepdims=True)
        p = jnp.where(mask, jnp.exp(st - m), 0.0)
        l = jnp.sum(p, axis=0, keepdims=True)
        inv_l = 1.0 / jnp.maximum(l, jnp.finfo(F32).tiny)
        pn = p * inv_l
        oct_ = _dot(vct_ref[0, 0, :, :rows], p.astype(BF16)) * inv_l
        oc_ref[...] = _unstack_heads(oct_.T).astype(oc_ref.dtype)
        psum = pn[:, 0:tq]
        for j in range(1, NSA_GROUP):
            psum = psum + pn[:, j * tq:(j + 1) * tq]
        p_hi = psum.astype(BF16)
        p_lo = (psum - p_hi.astype(F32)).astype(BF16)
        ov = ov_ref[:, :rows]
        imp_ref[...] = _dot(ov, p_hi) + _dot(ov, p_lo)

    for k in range(n_bucket):
        @pl.when((qi * n_bucket >= k * nq) & (qi * n_bucket < (k + 1) * nq))
        def _(k=k):
            attend((k + 1) * n16 // n_bucket)

    imp = imp_ref[...]
    jf = lax.broadcasted_iota(jnp.int32, imp.shape, 0).astype(F32)
    tl = t0 + lax.broadcasted_iota(jnp.int32, imp.shape, 1)
    tblk = jnp.right_shift(tl, SLC_LEN.bit_length() - 1).astype(F32)
    valid = jf <= tblk
    forced = (jf == 0.0) | (jf == tblk) | (jf == tblk - 1.0)
    sel = jnp.where(valid & forced, 1.0, 0.0)
    score = jnp.where(valid & jnp.logical_not(forced), imp, -jnp.inf)
    big = float(imp.shape[0])
    for _ in range(max(n_top - 3, 0)):
        mx = jnp.max(score, axis=0, keepdims=True)
        idx = jnp.min(jnp.where(score == mx, jf, big), axis=0, keepdims=True)
        hit = jf == idx
        sel = jnp.where(hit & (mx > -jnp.inf), 1.0, sel)
        score = jnp.where(hit, -jnp.inf, score)
    sel_ref[0, 0] = sel


def _cmpattn(q, kcmp, vcmpt, ov, *, tq, n_top):
    s = q.shape[0]
    g, n16 = kcmp.shape[1], kcmp.shape[2]
    n_slc = ov.shape[0]
    nq = s // tq
    gw = NSA_GROUP * LANE
    n_bucket = 4 if n16 % (4 * LANE) == 0 and nq % 4 == 0 else 1
    return pl.pallas_call(
        functools.partial(_cmpattn_body, tq=tq, n_top=n_top, n_bucket=n_bucket),
        grid=(g, nq),
        in_specs=[
            pl.BlockSpec((tq, gw), lambda a, i: (i, a)),
            pl.BlockSpec((1, 1, n16, LANE), lambda a, i: (0, a, 0, 0)),
            pl.BlockSpec((1, 1, LANE, n16), lambda a, i: (1, a, 0, 0)),
            _const_spec(ov.shape),
        ],
        out_specs=[
            pl.BlockSpec((tq, gw), lambda a, i: (i, a)),
            pl.BlockSpec((1, 1, n_slc, tq), lambda a, i: (a, i, 0, 0)),
        ],
        out_shape=[
            jax.ShapeDtypeStruct((s, g * gw), F32),
            jax.ShapeDtypeStruct((g, nq, n_slc, tq), F32),
        ],
        scratch_shapes=[pltpu.VMEM((n_slc, tq), F32)],
        compiler_params=_params("parallel", "arbitrary"),
        name="nsa_cmpattn",
    )(q, kcmp, vcmpt, ov)


def _selattn_body(q_ref, k_ref, vt_ref, sel_ref, o_ref, m_ref, l_ref, acc_ref, *, tq, kc, ng):
    qi = pl.program_id(1)
    gw = NSA_GROUP * LANE
    q3s = [_stack_heads(q_ref[:, b * gw:(b + 1) * gw]) for b in range(ng)]
    m_ref[...] = jnp.full_like(m_ref, NEG)
    l_ref[...] = jnp.zeros_like(l_ref)
    acc_ref[...] = jnp.zeros_like(acc_ref)
    bpc = kc // SLC_LEN
    n_full = (qi * tq) // kc

    def chunk(c, causal):
        start = pl.multiple_of(c * kc, kc)
        for b in range(ng):
            k = k_ref[pl.ds(start, kc), b * LANE:(b + 1) * LANE]
            st = _dot_nt(k, q3s[b])
            blk_rows = pl.ds(pl.multiple_of(c * bpc, bpc), bpc)
            srows = [sel_ref[b, u, blk_rows, :] for u in range(tq // sel_ref.shape[3])]
            pen = (jnp.concatenate(srows * NSA_GROUP, axis=1) - 1.0) * (-NEG)
            st = st + jnp.concatenate(
                [jnp.broadcast_to(pen[r:r + 1, :], (SLC_LEN, pen.shape[1])) for r in range(bpc)], axis=0)
            if causal:
                kpos = c * kc + lax.broadcasted_iota(jnp.int32, st.shape, 0)
                tpos = qi * tq + (lax.broadcasted_iota(jnp.int32, st.shape, 1) & (tq - 1))
                st = jnp.where(kpos <= tpos, st, NEG)
            m_old = m_ref[b]
            m_new = jnp.maximum(m_old, jnp.max(st, axis=0, keepdims=True))
            alpha = jnp.exp(m_old - m_new)
            p = jnp.exp(st - m_new)
            l_ref[b] = alpha * l_ref[b] + jnp.sum(p, axis=0, keepdims=True)
            acc_ref[b] = alpha * acc_ref[b] + _dot(vt_ref[b, c], p.astype(BF16))
            m_ref[b] = m_new

    def loop_body(c, carry):
        chunk(c, False)
        return carry

    lax.fori_loop(0, n_full, loop_body, 0)
    chunk(n_full, True)
    for b in range(ng):
        o_ref[:, b * gw:(b + 1) * gw] = _unstack_heads((acc_ref[b] / l_ref[b]).T).astype(o_ref.dtype)


def _selattn(q, ks, vst, sel, *, tq, ng):
    s = q.shape[0]
    g, _, n_slc, tsel = sel.shape
    _, nchunk, _, kc = vst.shape
    gw = NSA_GROUP * LANE
    sw = NSA_GROUP * tq
    resident = dict(pipeline_mode=pl.Buffered(1)) if g == ng else {}
    return pl.pallas_call(
        functools.partial(_selattn_body, tq=tq, kc=kc, ng=ng),
        grid=(g // ng, s // tq),
        in_specs=[
            pl.BlockSpec((tq, ng * gw), lambda a, i: (i, a)),
            pl.BlockSpec((s, ng * LANE), lambda a, i: (0, a), **resident),
            pl.BlockSpec((ng, nchunk, LANE, kc), lambda a, i: (a, 0, 0, 0), **resident),
            pl.BlockSpec((ng, tq // tsel, n_slc, tsel), lambda a, i: (a, i, 0, 0)),
        ],
        out_specs=pl.BlockSpec((tq, ng * gw), lambda a, i: (i, a)),
        out_shape=jax.ShapeDtypeStruct((s, g * gw), F32),
        scratch_shapes=[pltpu.VMEM((ng, 1, sw), F32), pltpu.VMEM((ng, 1, sw), F32),
                        pltpu.VMEM((ng, LANE, sw), F32)],
        compiler_params=_params("parallel", "arbitrary"),
        name="nsa_selattn",
    )(q, ks, vst, sel)


def _winattn_body(*refs, tq, nwb):
    q_ref, band_ref = refs[0], refs[1]
    k_refs = refs[2:3 + nwb]
    v_refs = refs[3 + nwb:4 + 2 * nwb]
    o_ref = refs[4 + 2 * nwb]
    qi = pl.program_id(1)
    q3 = _stack_heads(q_ref[...])
    k = jnp.concatenate([r[...] for r in k_refs], axis=0)
    v = jnp.concatenate([r[...] for r in v_refs], axis=0)
    s = _dot_nt(q3, k)
    col = lax.broadcasted_iota(jnp.int32, (1, s.shape[1]), 1)
    before_start = jnp.where(col < (nwb - qi) * tq, NEG, 0.0)
    s = s + jnp.concatenate([band_ref[...]] * NSA_GROUP, axis=0) + before_start
    m = jnp.max(s, axis=1, keepdims=True)
    p = jnp.exp(s - m)
    l = jnp.sum(p, axis=1, keepdims=True)
    o3 = _dot(p.astype(BF16), v) / l
    o_ref[...] = _unstack_heads(o3).astype(o_ref.dtype)


def _winattn(q, kw, vw, *, tq):
    s = q.shape[0]
    g = kw.shape[1] // LANE
    nwb = WINDOW // tq
    gw = NSA_GROUP * LANE

    def kv_spec(back):
        return pl.BlockSpec((tq, LANE), lambda a, i: (jnp.maximum(i - back, 0), a))

    kv_specs = [kv_spec(nwb - b) for b in range(nwb + 1)]
    dist = jnp.arange(tq)[:, None] - (jnp.arange((nwb + 1) * tq)[None, :] - nwb * tq)
    band = jnp.where((dist >= 0) & (dist < WINDOW), 0.0, NEG).astype(F32)
    return pl.pallas_call(
        functools.partial(_winattn_body, tq=tq, nwb=nwb),
        grid=(g, s // tq),
        in_specs=[pl.BlockSpec((tq, gw), lambda a, i: (i, a)), _const_spec(band.shape)] + kv_specs + kv_specs,
        out_specs=pl.BlockSpec((tq, gw), lambda a, i: (i, a)),
        out_shape=jax.ShapeDtypeStruct((s, g * gw), F32),
        compiler_params=_params("parallel", "parallel"),
        name="nsa_winattn",
    )(q, band, *([kw] * (nwb + 1)), *([vw] * (nwb + 1)))


def _outproj_body(*refs, mode):
    if mode == "nsa":
        x_ref, oc_ref, os_ref, ow_ref, gl_ref, qx_ref, mk_ref, mv_ref, wmix_ref, wmem_ref, o_ref = refs
        gate = jax.nn.sigmoid(gl_ref[...])
        parts = []
        for hd in range(MIX_HEADS):
            sl = slice(hd * LANE, (hd + 1) * LANE)
            acc = None
            for b, br in enumerate((oc_ref, os_ref, ow_ref)):
                c = b * MIX_HEADS + hd
                term = gate[:, c:c + 1] * br[:, sl]
                acc = term if acc is None else acc + term
            parts.append(acc.astype(BF16))
        ymix = jnp.concatenate(parts, axis=1)
    else:
        x_ref, ymix_ref, qx_ref, mk_ref, mv_ref, wmix_ref, wmem_ref, o_ref = refs
        ymix = ymix_ref[...]
    y = _dot(ymix, wmix_ref[...])
    for hd in range(MEM_HEADS):
        sl = slice(hd * LANE, (hd + 1) * LANE)
        s = _dot_nt(qx_ref[:, sl], mk_ref[:, sl]) * SCALE
        m = jnp.max(s, axis=1, keepdims=True)
        p = jnp.exp(s - m)
        p = p / jnp.sum(p, axis=1, keepdims=True)
        ymem = _dot(p.astype(BF16), mv_ref[:, sl]).astype(BF16)
        y = y + _dot(ymem, wmem_ref[sl, :])
    o_ref[...] = x_ref[...] + y


def _outproj(x, mix_args, qx, mk, mv, wmix, wmem, *, mode, tm):
    s, d = x.shape
    row = lambda w: pl.BlockSpec((tm, w), lambda i: (i, 0))
    mix_specs = [row(a.shape[1]) for a in mix_args]
    return pl.pallas_call(
        functools.partial(_outproj_body, mode=mode),
        grid=(s // tm,),
        in_specs=[row(d)] + mix_specs + [row(qx.shape[1]), _const_spec(mk.shape), _const_spec(mv.shape),
                                         _const_spec(wmix.shape), _const_spec(wmem.shape)],
        out_specs=row(d),
        out_shape=jax.ShapeDtypeStruct((s, d), F32),
        compiler_params=_params("parallel"),
        name="outproj",
    )(x, *mix_args, qx, mk, mv, wmix, wmem)


def _pad_heads_cols(w, n_heads):
    d = w.shape[0]
    w = w.reshape(d, n_heads, HEAD_DIM)
    return jnp.pad(w, ((0, 0), (0, 0), (0, LANE - HEAD_DIM))).reshape(d, n_heads * LANE)


def _pad_heads_rows(w, n_heads):
    d = w.shape[1]
    w = w.reshape(n_heads, HEAD_DIM, d)
    return jnp.pad(w, ((0, 0), (0, LANE - HEAD_DIM), (0, 0))).reshape(n_heads * LANE, d)


def _pad_cols(w, width):
    return jnp.pad(w, ((0, 0), (0, width - w.shape[1])))


def _rope_tables(s):
    half = ROT_DIM // 2
    inv = ROPE_THETA ** (-jnp.arange(half, dtype=F32) / half)
    ang = jnp.arange(s, dtype=F32)[:, None] * inv[None, :]
    cos, sin = jnp.cos(ang), jnp.sin(ang)
    z = lambda n: jnp.zeros((s, n), F32)
    cos_t = jnp.concatenate([cos, cos, jnp.ones((s, LANE - ROT_DIM), F32)], axis=1)
    sa_t = jnp.concatenate([z(half), sin, z(LANE - ROT_DIM)], axis=1)
    sb_t = jnp.concatenate([-sin, z(LANE - half)], axis=1)
    return cos_t, sa_t, sb_t


def _pick(n, candidates):
    for c in candidates:
        if n % c == 0:
            return c
    raise ValueError(f"no tile size for {n}")


def kernel(x, mem, ffn1_norm, ffn1_w_in, ffn1_w_out, mix_norm, mix_w_out, mem_norm, mem_w_kv, fox_w_in,
           fox_b_f, conv_w_in, conv_w, nsa_w_in, nsa_cmp_pos, nsa_cmp_w1, nsa_cmp_w2, ffn2_norm, ffn2_w_in,
           ffn2_w_out, final_norm):
    b, s, d = x.shape
    assert b == 1
    depth = ffn1_norm.shape[0]
    mix_w = MIX_HEADS * HEAD_DIM
    mem_w = MEM_HEADS * HEAD_DIM
    kv_w = NSA_KV_HEADS * HEAD_DIM
    d_ff = ffn1_w_out.shape[1]
    tm_ffn = _pick(s, (512, 256, 128))
    fc = d_ff
    tm = _pick(s, (512, 256, 128))
    tq_fox = _pick(s, (512, 256, 128))
    tq_nsa = 128
    kc_sel = _pick(s, (512, 256, 128))
    tq_win = _pick(s, (256, 128))
    m_len = mem.shape[1]

    xs = x[0]
    mem_s = mem[0]
    rope_t = None
    for i in range(depth):
        xs = _ffn(xs, ffn1_norm[i], ffn1_w_in[i].astype(BF16), ffn1_w_out[i].astype(BF16), tm=tm_ffn, fc=fc)

        wkv = mem_w_kv[i]
        wkv_p = jnp.concatenate([_pad_heads_cols(wkv[:, :mem_w], MEM_HEADS),
                                 _pad_heads_cols(wkv[:, mem_w:], MEM_HEADS)], axis=1).astype(BF16)
        mk, mv = _proj(mem_s, mem_norm, wkv_p, [(MEM_HEADS * LANE, BF16, False)] * 2, tm=m_len)

        w_out = mix_w_out[i]
        wmem = _pad_heads_rows(w_out[mix_w:], MEM_HEADS).astype(BF16)
        kind, j = i % N_MIXERS, i // N_MIXERS
        if kind == 0:
            w = fox_w_in[j]
            wq, wk, wv = (w[:, a * mix_w:(a + 1) * mix_w] for a in range(3))
            wf = w[:, 3 * mix_w:3 * mix_w + MIX_HEADS]
            wqx = w[:, 3 * mix_w + MIX_HEADS:]
            wp = jnp.concatenate([_pad_heads_cols(wq, MIX_HEADS), _pad_heads_cols(wk, MIX_HEADS),
                                  _pad_heads_cols(wv, MIX_HEADS), _pad_heads_cols(wqx, MEM_HEADS),
                                  _pad_cols(wf, LANE)], axis=1).astype(BF16)
            hw = MIX_HEADS * LANE
            q, k, v, qx, f = _proj(xs, mix_norm[i], wp,
                                   [(hw, BF16, False)] * 3 + [(MEM_HEADS * LANE, BF16, False), (LANE, F32, False)],
                                   tm=tm)
            ft = _fgate(f, _pad_cols(fox_b_f[j].reshape(1, MIX_HEADS), LANE), tk=tq_fox)
            ymix = _fox(q, k, v, ft, tq=tq_fox)
            wmix = _pad_heads_rows(w_out[:mix_w], MIX_HEADS).astype(BF16)
            xs = _outproj(xs, [ymix], qx, mk, mv, wmix, wmem, mode="plain", tm=tm)
        elif kind == 1:
            w = conv_w_in[j]
            wp = jnp.concatenate([w[:, :3 * mix_w], _pad_heads_cols(w[:, 3 * mix_w:], MEM_HEADS)],
                                 axis=1).astype(BF16)
            bg, cg, v, qx = _proj(xs, mix_norm[i], wp,
                                  [(mix_w, F32, False)] * 3 + [(MEM_HEADS * LANE, BF16, False)], tm=tm)
            ymix = _conv(bg, cg, v, conv_w[j], tm=tm)
            xs = _outproj(xs, [ymix], qx, mk, mv, w_out[:mix_w].astype(BF16), wmem, mode="plain", tm=tm)
        else:
            w = nsa_w_in[j]
            off = mix_w
            wq = w[:, :mix_w]
            kvs = []
            for _ in range(6):
                kvs.append(_pad_heads_cols(w[:, off:off + kv_w], NSA_KV_HEADS))
                off += kv_w
            wgl = w[:, off:off + 3 * MIX_HEADS]
            wqx = w[:, off + 3 * MIX_HEADS:]
            wp = jnp.concatenate([_pad_heads_cols(wq, MIX_HEADS)] + kvs +
                                 [_pad_heads_cols(wqx, MEM_HEADS), _pad_cols(wgl, LANE)], axis=1).astype(BF16)
            if rope_t is None:
                rope_t = _rope_tables(s)
            gwid = NSA_KV_HEADS * LANE
            segs = [(MIX_HEADS * LANE, BF16, True)]
            segs += [(gwid, BF16, a % 2 == 0) for a in range(6)]
            segs += [(MEM_HEADS * LANE, BF16, False), (LANE, F32, False)]
            q, kc, vc, ks_, vs_, kw, vw, qx, gl = _proj(xs, mix_norm[i], wp, segs, rope_t, tm=tm)

            n16 = s // CMP_STRIDE
            sub = CMP_STRIDE

            def to_t16(t):
                t = t.reshape(n16, sub, NSA_KV_HEADS, LANE)[..., :HEAD_DIM]
                return t.transpose(2, 0, 1, 3).reshape(NSA_KV_HEADS, n16, sub * HEAD_DIM)

            t16 = jnp.stack([to_t16(kc), to_t16(vc)])
            w1 = nsa_cmp_w1[j].astype(BF16)
            pe8 = jnp.zeros((2, SUBLANE, CMP_LEN * HEAD_DIM), F32).at[:, 0].set(
                nsa_cmp_pos[j].reshape(2, CMP_LEN * HEAD_DIM)).astype(BF16)
            w2d = jnp.pad(nsa_cmp_w2[j], ((0, 0), (0, 0), (0, LANE - HEAD_DIM))).astype(BF16)
            cmp_n, cmp_t = _cmp(t16, w1, pe8, w2d)

            n_slc = s // SLC_LEN
            n_top = min(SLC_TOPN, n_slc)
            jj = jnp.arange(n_slc)[:, None] * SLC_LEN
            nn = jnp.arange(n16)[None, :] * CMP_STRIDE
            ov = ((nn < jj + SLC_LEN) & (nn + CMP_LEN > jj) & (nn + CMP_LEN <= s)).astype(BF16)
            oc, sel = _cmpattn(q, cmp_n, cmp_t, ov, tq=tq_nsa, n_top=n_top)

            vst = vs_.reshape(s // kc_sel, kc_sel, NSA_KV_HEADS, LANE).transpose(2, 0, 3, 1)
            osel = _selattn(q, ks_, vst, sel, tq=2 * tq_nsa, ng=NSA_KV_HEADS)
            ow = _winattn(q, kw, vw, tq=tq_win)
            wmix = _pad_heads_rows(w_out[:mix_w], MIX_HEADS).astype(BF16)
            xs = _outproj(xs, [oc, osel, ow, gl], qx, mk, mv, wmix, wmem, mode="nsa", tm=tm)

        last = i == depth - 1
        xs = _ffn(xs, ffn2_norm[i], ffn2_w_in[i].astype(BF16), ffn2_w_out[i].astype(BF16),
                  final_norm if last else None, tm=tm_ffn, fc=fc)
    return xs[None]
```
